```python
import math
import jax
import jax.numpy as jnp
from jax import lax
import numpy as np

D_MODEL = 2048
BATCH = 32
SEQ = 256
DEPTH = 4
DEC_BATCH = 4
DEC_SEQ = 4096
PAST_LEN = 512

GRID_W = 64
N_MIXERS = 3
N_LAYERS_A = (DEPTH + 2) // 3
N_LAYERS_B = (DEPTH + 1) // 3
N_LAYERS_C = DEPTH // 3
HG_HEADS = 16
HG_DK = D_MODEL // HG_HEADS
HG_DV = D_MODEL // HG_HEADS
HG_CHUNK = 16
DA_HEADS = 16
DA_QK_DIM = D_MODEL // (2 * DA_HEADS)
DA_V_DIM = 2 * DA_QK_DIM
NA_HEADS = 16
NA_HEAD_DIM = D_MODEL // NA_HEADS
NA_WIN_ROWS = 8
NA_WIN_COLS = 16
MOE_GROUPS = 4
MOE_EXPERTS_PER_GROUP = 8
MOE_EXPERTS = MOE_GROUPS * MOE_EXPERTS_PER_GROUP
MOE_TOP_K = 2
MOE_D_EXPERT = 512
MOE_BLOCK = 128
Q_BLOCK = 128
ROPE_BASE = 10000.0
NORM_EPS = 1e-6
N_MOD = 6

kernel_name = 'hybrid_dit_hgrn2_diffattn_natten_hmoe_step'


def rms_norm(x, gain):
    xf = x.astype(jnp.float32)
    y = xf * lax.rsqrt(jnp.mean(xf * xf, axis=-1, keepdims=True) + NORM_EPS)
    return (y * gain.astype(jnp.float32)).astype(x.dtype)


def ada_terms(cond, w_ada, b_ada):
    mod = (jax.nn.silu(cond) @ w_ada + b_ada)[..., None, :]
    return jnp.split(mod, N_MOD, axis=-1)


def axial_rope(n, dim):
    quarter = dim // 4
    inv = jnp.power(ROPE_BASE, -jnp.arange(quarter, dtype=jnp.float32) / quarter)
    t = jnp.arange(n)
    row = (t // GRID_W).astype(jnp.float32)
    col = (t % GRID_W).astype(jnp.float32)
    ang = jnp.concatenate([row[:, None] * inv, col[:, None] * inv], axis=-1)
    return jnp.cos(ang), jnp.sin(ang)


def apply_rope(x, cos, sin):
    half = x.shape[-1] // 2
    lead = (cos.shape[0],) + (1,) * (x.ndim - 3) + (half,)
    c, s = cos.reshape(lead), sin.reshape(lead)
    x1 = x[..., :half].astype(jnp.float32)
    x2 = x[..., half:].astype(jnp.float32)
    return jnp.concatenate([x1 * c - x2 * s, x1 * s + x2 * c], axis=-1).astype(x.dtype)


def query_blocks(q):
    b, n = q.shape[:2]
    return jnp.moveaxis(q.reshape((b, n // Q_BLOCK, Q_BLOCK) + q.shape[2:]), 1, 0)


def merge_blocks(o):
    nb, b, qb = o.shape[:3]
    return jnp.moveaxis(o, 0, 1).reshape((b, nb * qb) + o.shape[3:])


def softmax_attention(q, k, v):
    scale = q.shape[-1] ** -0.5

    def block(qi):
        s = jnp.einsum('bqhd,bkhd->bhqk', qi, k).astype(jnp.float32) * scale
        p = jax.nn.softmax(s, axis=-1).astype(v.dtype)
        return jnp.einsum('bhqk,bkhd->bqhd', p, v)

    return merge_blocks(lax.map(block, query_blocks(q)))


def diff_attention(q, k, v, lam):
    scale = q.shape[-1] ** -0.5

    def block(qi):
        s = jnp.einsum('bqhcd,bkhcd->bhcqk', qi, k).astype(jnp.float32) * scale
        p = jax.nn.softmax(s, axis=-1)
        a = (p[:, :, 0] - lam * p[:, :, 1]).astype(v.dtype)
        return jnp.einsum('bhqk,bkhd->bqhd', a, v)

    return merge_blocks(lax.map(block, query_blocks(q)))


def diff_lambda(lam_params, layer_idx):
    lam_init = 0.8 - 0.6 * math.exp(-0.3 * layer_idx)
    lp = lam_params.astype(jnp.float32)
    lam = jnp.exp(jnp.sum(lp[0] * lp[1])) - jnp.exp(jnp.sum(lp[2] * lp[3])) + lam_init
    return lam, lam_init


def diff_project(h, w_qkv):
    b, n, _ = h.shape
    q, k, v = jnp.split(h @ w_qkv, 3, axis=-1)
    return (q.reshape(b, n, DA_HEADS, 2, DA_QK_DIM), k.reshape(b, n, DA_HEADS, 2, DA_QK_DIM),
            v.reshape(b, n, DA_HEADS, DA_V_DIM))


def diff_output(o, subln, lam_init, w_out):
    b, n = o.shape[:2]
    o = rms_norm(o, subln) * (1.0 - lam_init)
    return o.reshape(b, n, D_MODEL) @ w_out


def na_project(h, w_qkv):
    b, n, _ = h.shape
    q, k, v = jnp.split(h @ w_qkv, 3, axis=-1)
    shape = (b, n, NA_HEADS, NA_HEAD_DIM)
    return q.reshape(shape), k.reshape(shape), v.reshape(shape)


def neighbourhood_attention(q, k, v, ck, cv, rpb):
    b, n, h, d = q.shape
    rows = n // GRID_W
    wr = min(NA_WIN_ROWS, rows)
    scale = d ** -0.5
    qg = q.reshape(b, rows, GRID_W, h, d)
    kg = k.reshape(b, rows, GRID_W, h, d)
    vg = v.reshape(b, rows, GRID_W, h, d)
    col = jnp.arange(GRID_W)
    col0 = jnp.clip(col - NA_WIN_COLS // 2, 0, GRID_W - NA_WIN_COLS)
    col_ok = (col[None, :] >= col0[:, None]) & (col[None, :] < col0[:, None] + NA_WIN_COLS)
    dc_idx = jnp.clip(col[None, :] - col[:, None], -(NA_WIN_COLS - 1), NA_WIN_COLS - 1) + NA_WIN_COLS - 1
    rpb_f = rpb.astype(jnp.float32)
    n_lat = wr * GRID_W

    def row_block(r):
        r0 = jnp.clip(r - wr // 2, 0, rows - wr)
        qb = lax.dynamic_index_in_dim(qg, r, axis=1, keepdims=False)
        kb = lax.dynamic_slice_in_dim(kg, r0, wr, axis=1)
        vb = lax.dynamic_slice_in_dim(vg, r0, wr, axis=1)
        dr_idx = r0 + jnp.arange(wr) - r + NA_WIN_ROWS - 1
        bias = rpb_f[:, dr_idx[None, :, None], dc_idx[:, None, :]]
        s_lat = jnp.einsum('bqhd,brkhd->bhqrk', qb, kb).astype(jnp.float32) * scale + bias
        s_lat = jnp.where(col_ok[:, None, :], s_lat, -jnp.inf)
        s_ctx = jnp.einsum('bqhd,bchd->bhqc', qb, ck).astype(jnp.float32) * scale
        s = jnp.concatenate([s_lat.reshape(b, h, GRID_W, n_lat), s_ctx], axis=-1)
        p = jax.nn.softmax(s, axis=-1).astype(v.dtype)
        p_lat = p[..., :n_lat].reshape(b, h, GRID_W, wr, GRID_W)
        return (jnp.einsum('bhqrk,brkhd->bqhd', p_lat, vb)
                + jnp.einsum('bhqc,bchd->bqhd', p[..., n_lat:], cv))

    o = lax.map(row_block, jnp.arange(rows))
    return jnp.moveaxis(o, 0, 1).reshape(b, n, h, d)


def gla_scan(q, k, v, logf, s0):
    b, n, h, _ = q.shape
    nc = n // HG_CHUNK

    def chunks(t):
        return t.reshape(b, nc, HG_CHUNK, h, t.shape[-1]).transpose(1, 0, 3, 2, 4)

    tri = jnp.tril(jnp.ones((HG_CHUNK, HG_CHUNK), dtype=bool))

    def step(s, inp):
        qc, kc, vc, gc = inp
        cum = jnp.cumsum(gc, axis=2)
        tot = cum[:, :, -1]
        o_inter = jnp.einsum('bhtk,bhkv->bhtv', qc * jnp.exp(cum), s)
        rel = jnp.where(tri[:, :, None], cum[:, :, :, None, :] - cum[:, :, None, :, :], -jnp.inf)
        att = jnp.einsum('bhtsk,bhsk->bhts', qc[:, :, :, None, :] * jnp.exp(rel), kc)
        o = o_inter + jnp.einsum('bhts,bhsv->bhtv', att, vc)
        s_new = s * jnp.exp(tot)[..., None] + jnp.einsum(
            'bhsk,bhsv->bhkv', kc * jnp.exp(tot[:, :, None] - cum), vc)
        return s_new, o

    s_fin, o = lax.scan(step, s0, (chunks(q), chunks(k), chunks(v), chunks(logf)))
    return o.transpose(1, 0, 3, 2, 4).reshape(b, n, h, -1), s_fin


def hgrn_lower_bounds(logits):
    p = jax.nn.softmax(logits.astype(jnp.float32), axis=0)
    cum = jnp.cumsum(p, axis=0)
    return cum - cum[0:1]


def hgrn2_mixer(h, w_in, lb, gnorm, w_out, s0_f, s0_b):
    b, n, d = h.shape
    q, i_in, z_f, z_b, g = jnp.split(h @ w_in, 5, axis=-1)

    def heads(t):
        return t.reshape(b, n, HG_HEADS, HG_DK).astype(jnp.float32)

    q = heads(jax.nn.silu(q)) * HG_DK ** -0.5
    v = heads(i_in)
    lbh = lb.reshape(HG_HEADS, HG_DK)

    def forget(z):
        z = heads(z)
        logf = jnp.logaddexp(jnp.log(lbh), jnp.log1p(-lbh) + jax.nn.log_sigmoid(z))
        return logf, (1.0 - lbh) * jax.nn.sigmoid(-z)

    lf_f, k_f = forget(z_f)
    lf_b, k_b = forget(z_b)
    o_f, s_f = gla_scan(q, k_f, v, lf_f, s0_f)

    def flip(t):
        return jnp.flip(t, axis=1)

    o_b, s_b = gla_scan(flip(q), flip(k_b), flip(v), flip(lf_b), s0_b)
    o = rms_norm(o_f + flip(o_b), gnorm).astype(h.dtype).reshape(b, n, d)
    return (o * jax.nn.silu(g)) @ w_out, s_f, s_b


def grouped_experts(xf, e_idx, gate, w_gate, w_up, w_down):
    t, k = e_idx.shape
    n_exp = w_gate.shape[0]
    n_assign = t * k
    flat_e = e_idx.reshape(-1)
    flat_tok = jnp.arange(n_assign, dtype=jnp.int32) // k
    order = jnp.argsort(flat_e)
    sorted_e = flat_e[order]
    counts = jnp.bincount(flat_e, length=n_exp)
    padded = (counts + MOE_BLOCK - 1) // MOE_BLOCK * MOE_BLOCK
    seg_end = jnp.cumsum(padded)
    seg_start = seg_end - padded
    raw_start = jnp.cumsum(counts) - counts
    dest = seg_start[sorted_e] + jnp.arange(n_assign, dtype=jnp.int32) - raw_start[sorted_e]
    n_blocks = -(-(n_assign + n_exp * (MOE_BLOCK - 1)) // MOE_BLOCK)
    cap = n_blocks * MOE_BLOCK
    slot_tok = jnp.zeros((cap,), jnp.int32).at[dest].set(flat_tok[order])
    block_e = jnp.minimum(jnp.searchsorted(seg_end, jnp.arange(n_blocks) * MOE_BLOCK, side='right'), n_exp - 1)
    xb = xf[slot_tok].reshape(n_blocks, MOE_BLOCK, xf.shape[-1])

    def expert_block(args):
        xi, e = args
        hmid = jax.nn.silu(xi @ w_gate[e]) * (xi @ w_up[e])
        return hmid @ w_down[e]

    yb = lax.map(expert_block, (xb, block_e)).reshape(cap, xf.shape[-1])
    contrib = yb[dest] * gate.reshape(-1)[order][:, None].astype(yb.dtype)
    return jnp.zeros_like(xf).at[flat_tok[order]].add(contrib)


def hier_moe(x, wg_r, bg_r, we_r, be_r, w_gate, w_up, w_down):
    b, n, d = x.shape
    xf = x.reshape(b * n, d)
    g_prob = jax.nn.softmax((xf @ wg_r).astype(jnp.float32) + bg_r.astype(jnp.float32), axis=-1)
    p_grp, grp = lax.top_k(g_prob, 1)
    e_logits = ((xf @ we_r).astype(jnp.float32) + be_r.astype(jnp.float32)).reshape(
        b * n, MOE_GROUPS, MOE_EXPERTS_PER_GROUP)
    e_in = jnp.take_along_axis(e_logits, grp[:, :, None], axis=1)[:, 0]
    p_exp, e_loc = lax.top_k(jax.nn.softmax(e_in, axis=-1), MOE_TOP_K)
    gate = p_grp * p_exp / jnp.sum(p_exp, axis=-1, keepdims=True)
    e_idx = grp * MOE_EXPERTS_PER_GROUP + e_loc
    return grouped_experts(xf, e_idx, gate, w_gate, w_up, w_down).reshape(b, n, d)


def setup_inputs(seed: int = 0) -> dict:
    key = jax.random.key(seed)
    keys = iter(jax.random.split(key, 40))

    def nrm(shape, scale):
        return jax.random.normal(next(keys), shape, jnp.float32) * scale

    d = D_MODEL
    s_in = d ** -0.5
    return {
        'x_prompt': nrm((BATCH, SEQ, d), 1.0),
        'x_sample': nrm((DEC_BATCH, DEC_SEQ, d), 1.0),
        'state_hgrn': nrm((DEC_BATCH, N_LAYERS_A, 2, HG_HEADS, HG_DK, HG_DV), 0.5),
        'cache_diff_k': nrm((DEC_BATCH, N_LAYERS_B, PAST_LEN, DA_HEADS, 2, DA_QK_DIM), 1.0),
        'cache_diff_v': nrm((DEC_BATCH, N_LAYERS_B, PAST_LEN, DA_HEADS, DA_V_DIM), 1.0),
        'cache_na_k': nrm((DEC_BATCH, N_LAYERS_C, PAST_LEN, NA_HEADS, NA_HEAD_DIM), 1.0),
        'cache_na_v': nrm((DEC_BATCH, N_LAYERS_C, PAST_LEN, NA_HEADS, NA_HEAD_DIM), 1.0),
        'c': nrm((DEC_BATCH, d), 1.0),
        'c_ctx': nrm((d,), 1.0),
        'norm_mix': 1.0 + nrm((DEPTH, d), 0.02),
        'norm_ffn': 1.0 + nrm((DEPTH, d), 0.02),
        'w_ada': nrm((DEPTH, d, N_MOD * d), 0.5 * s_in),
        'b_ada': nrm((DEPTH, N_MOD * d), 0.01),
        'router_group_w': nrm((DEPTH, d, MOE_GROUPS), s_in),
        'router_group_b': nrm((DEPTH, MOE_GROUPS), 0.01),
        'router_expert_w': nrm((DEPTH, d, MOE_EXPERTS), s_in),
        'router_expert_b': nrm((DEPTH, MOE_EXPERTS), 0.01),
        'expert_w_gate': nrm((DEPTH, MOE_EXPERTS, d, MOE_D_EXPERT), s_in),
        'expert_w_up': nrm((DEPTH, MOE_EXPERTS, d, MOE_D_EXPERT), s_in),
        'expert_w_down': nrm((DEPTH, MOE_EXPERTS, MOE_D_EXPERT, d), MOE_D_EXPERT ** -0.5),
        'hg_w_in': nrm((N_LAYERS_A, d, 5 * d), s_in),
        'hg_lb_logits': nrm((DEPTH, d), 0.5),
        'hg_gnorm': 1.0 + nrm((N_LAYERS_A, HG_DV), 0.02),
        'hg_w_out': nrm((N_LAYERS_A, d, d), s_in),
        'da_w_qkv': nrm((N_LAYERS_B, d, 3 * d), s_in),
        'da_lambda': nrm((N_LAYERS_B, 4, DA_QK_DIM), 0.1),
        'da_subln': 1.0 + nrm((N_LAYERS_B, DA_V_DIM), 0.02),
        'da_w_out': nrm((N_LAYERS_B, d, d), s_in),
        'na_w_qkv': nrm((N_LAYERS_C, d, 3 * d), s_in),
        'na_rpb': nrm((N_LAYERS_C, NA_HEADS, 2 * NA_WIN_ROWS - 1, 2 * NA_WIN_COLS - 1), 0.1),
        'na_w_out': nrm((N_LAYERS_C, d, d), s_in),
        'final_norm': 1.0 + nrm((d,), 0.02),
    }


def reference(x_prompt, x_sample, state_hgrn, cache_diff_k, cache_diff_v, cache_na_k, cache_na_v, c,
              c_ctx, norm_mix, norm_ffn, w_ada, b_ada, router_group_w, router_group_b, router_expert_w,
              router_expert_b, expert_w_gate, expert_w_up, expert_w_down, hg_w_in, hg_lb_logits, hg_gnorm,
              hg_w_out, da_w_qkv, da_lambda, da_subln, da_w_out, na_w_qkv, na_rpb, na_w_out, final_norm):
    lb_all = hgrn_lower_bounds(hg_lb_logits)
    cos, sin = axial_rope(x_sample.shape[1], DA_QK_DIM)
    xp, xs = x_prompt, x_sample
    new_hg, new_dk, new_dv, new_nk, new_nv = [], [], [], [], []
    for i in range(DEPTH):
        kind, j = i % N_MIXERS, i // N_MIXERS
        pm_sh, pm_sc, pm_g, pf_sh, pf_sc, pf_g = ada_terms(c_ctx, w_ada[i], b_ada[i])
        sm_sh, sm_sc, sm_g, sf_sh, sf_sc, sf_g = ada_terms(c, w_ada[i], b_ada[i])
        hp = rms_norm(xp, norm_mix[i]) * (1.0 + pm_sc) + pm_sh
        hs = rms_norm(xs, norm_mix[i]) * (1.0 + sm_sc) + sm_sh
        if kind == 0:
            zero = jnp.zeros((xp.shape[0], HG_HEADS, HG_DK, HG_DV), jnp.float32)
            mp, s_f, s_b = hgrn2_mixer(hp, hg_w_in[j], lb_all[i], hg_gnorm[j], hg_w_out[j], zero, zero)
            new_hg.append(jnp.stack([s_f, s_b], axis=1).astype(xp.dtype))
            ms, _, _ = hgrn2_mixer(hs, hg_w_in[j], lb_all[i], hg_gnorm[j], hg_w_out[j],
                                   state_hgrn[:, j, 0].astype(jnp.float32),
                                   state_hgrn[:, j, 1].astype(jnp.float32))
        elif kind == 1:
            lam, lam_init = diff_lambda(da_lambda[j], i)
            q, k, v = diff_project(hp, da_w_qkv[j])
            mp = diff_output(diff_attention(q, k, v, lam), da_subln[j], lam_init, da_w_out[j])
            new_dk.append(k)
            new_dv.append(v)
            q, k, v = diff_project(hs, da_w_qkv[j])
            q, k = apply_rope(q, cos, sin), apply_rope(k, cos, sin)
            k = jnp.concatenate([k, cache_diff_k[:, j]], axis=1)
            v = jnp.concatenate([v, cache_diff_v[:, j]], axis=1)
            ms = diff_output(diff_attention(q, k, v, lam), da_subln[j], lam_init, da_w_out[j])
        else:
            q, k, v = na_project(hp, na_w_qkv[j])
            b, n = hp.shape[:2]
            mp = softmax_attention(q, k, v).reshape(b, n, D_MODEL) @ na_w_out[j]
            new_nk.append(k)
            new_nv.append(v)
            q, k, v = na_project(hs, na_w_qkv[j])
            b, n = hs.shape[:2]
            o = neighbourhood_attention(q, k, v, cache_na_k[:, j], cache_na_v[:, j], na_rpb[j])
            ms = o.reshape(b, n, D_MODEL) @ na_w_out[j]
        xp = xp + pm_g * mp
        xs = xs + sm_g * ms
        hp = rms_norm(xp, norm_ffn[i]) * (1.0 + pf_sc) + pf_sh
        hs = rms_norm(xs, norm_ffn[i]) * (1.0 + sf_sc) + sf_sh
        xp = xp + pf_g * hier_moe(hp, router_group_w[i], router_group_b[i], router_expert_w[i],
                                  router_expert_b[i], expert_w_gate[i], expert_w_up[i], expert_w_down[i])
        xs = xs + sf_g * hier_moe(hs, router_group_w[i], router_group_b[i], router_expert_w[i],
                                  router_expert_b[i], expert_w_gate[i], expert_w_up[i], expert_w_down[i])
    y_prompt = rms_norm(xp, final_norm)
    y_sample = rms_norm(xs, final_norm)
    return (y_prompt, y_sample, jnp.stack(new_hg, axis=1), jnp.stack(new_dk, axis=1),
            jnp.stack(new_dv, axis=1), jnp.stack(new_nk, axis=1), jnp.stack(new_nv, axis=1))
```

```python
import functools
import math

import jax
import jax.numpy as jnp
from jax import lax
from jax.experimental import pallas as pl
from jax.experimental.pallas import tpu as pltpu

D_MODEL = 2048
BATCH = 32
SEQ = 256
DEPTH = 4
DEC_BATCH = 4
DEC_SEQ = 4096
PAST_LEN = 512
GRID_W = 64
N_MIXERS = 3
HEADS = 16
HEAD_DIM = D_MODEL // HEADS
DA_QK_DIM = 64
NA_WIN_ROWS = 8
NA_WIN_COLS = 16
MOE_GROUPS = 4
MOE_EXPERTS_PER_GROUP = 8
MOE_EXPERTS = MOE_GROUPS * MOE_EXPERTS_PER_GROUP
MOE_D_EXPERT = 512
ROPE_BASE = 10000.0
NORM_EPS = 1e-6
N_MOD = 6

T_PROMPT = BATCH * SEQ
T_SAMPLE = DEC_BATCH * DEC_SEQ
T_ALL = T_PROMPT + T_SAMPLE
SEG = DEC_SEQ
N_SEG = T_ALL // SEG
MOD_ROWS = 8

LANES = 128
VMEM_LIMIT = 56 * 1024 * 1024

TM = 512
TN = 1024
MOE_ROWS = 256
HG_TILE = 128
NA_TILE_ROWS = 8
NA_KEY_ROWS = 16

F32 = jnp.float32
BF16 = jnp.bfloat16


def _cparams(n_axes):
    return pltpu.CompilerParams(dimension_semantics=("arbitrary",) * n_axes, vmem_limit_bytes=VMEM_LIMIT)


def _split_bf16(x):
    hi = x.astype(BF16)
    lo = (x - hi.astype(F32)).astype(BF16)
    return hi, lo


def _ada_kernel(cond_ref, w_ref, b_ref, o_ref):
    a = cond_ref[...]
    a = a / (1.0 + jnp.exp(-a))
    a_hi, a_lo = _split_bf16(a)
    w_hi, w_lo = _split_bf16(w_ref[...])
    acc = jnp.dot(a_hi, w_hi, preferred_element_type=F32)
    acc += jnp.dot(a_lo, w_hi, preferred_element_type=F32)
    acc += jnp.dot(a_hi, w_lo, preferred_element_type=F32)
    o_ref[...] = acc + b_ref[...]


def ada_all(cond8, w_ada, b_ada):
    depth, d, n = w_ada.shape
    tn = 1024
    return pl.pallas_call(
        _ada_kernel,
        grid=(depth, n // tn),
        in_specs=[
            pl.BlockSpec((MOD_ROWS, d), lambda l, j: (0, 0)),
            pl.BlockSpec((None, d, tn), lambda l, j: (l, 0, j)),
            pl.BlockSpec((None, 1, tn), lambda l, j: (l, 0, j)),
        ],
        out_specs=pl.BlockSpec((None, MOD_ROWS, tn), lambda l, j: (l, 0, j)),
        out_shape=jax.ShapeDtypeStruct((depth, MOD_ROWS, n), F32),
        compiler_params=_cparams(2),
        name="ada",
    )(cond8, w_ada, b_ada.reshape(depth, 1, n))


def _norm_mod(x, gain, mod, sh_idx, sc_idx):
    ms = jnp.mean(x * x, axis=-1, keepdims=True)
    y = x * lax.rsqrt(ms + NORM_EPS) * gain
    return y * (1.0 + mod[sc_idx:sc_idx + 1, :]) + mod[sh_idx:sh_idx + 1, :]


def _norm_kernel(x_ref, gain_ref, mod_ref, h_ref, *, sh_idx, sc_idx):
    h_ref[...] = _norm_mod(x_ref[...], gain_ref[...], mod_ref[...], sh_idx, sc_idx).astype(BF16)


def _route(logits):
    lane = lax.broadcasted_iota(jnp.int32, logits.shape, 1)
    neg = jnp.float32(-jnp.inf)
    big = jnp.int32(LANES)
    lg = jnp.where(lane < MOE_GROUPS, logits, neg)
    mg = jnp.max(lg, axis=1, keepdims=True)
    grp = jnp.min(jnp.where(lg == mg, lane, big), axis=1, keepdims=True)
    p_grp = 1.0 / jnp.sum(jnp.exp(lg - mg), axis=1, keepdims=True)
    lo = MOE_GROUPS + grp * MOE_EXPERTS_PER_GROUP
    le = jnp.where((lane >= lo) & (lane < lo + MOE_EXPERTS_PER_GROUP), logits, neg)
    m1 = jnp.max(le, axis=1, keepdims=True)
    i1 = jnp.min(jnp.where(le == m1, lane, big), axis=1, keepdims=True)
    le2 = jnp.where(lane == i1, neg, le)
    m2 = jnp.max(le2, axis=1, keepdims=True)
    i2 = jnp.min(jnp.where(le2 == m2, lane, big), axis=1, keepdims=True)
    p2 = jnp.exp(m2 - m1)
    g1 = p_grp / (1.0 + p2)
    g2 = p_grp * p2 / (1.0 + p2)
    e1 = (i1 - MOE_GROUPS).astype(F32)
    e2 = (i2 - MOE_GROUPS).astype(F32)
    out = jnp.where(lane == 0, g1, 0.0)
    out = jnp.where(lane == 1, g2, out)
    out = jnp.where(lane == 2, e1, out)
    out = jnp.where(lane == 3, e2, out)
    return out


def _norm_route_kernel(x_ref, gain_ref, mod_ref, wr_hi_ref, wr_lo_ref, br_ref, h_ref, r_ref, *, sh_idx, sc_idx):
    h = _norm_mod(x_ref[...], gain_ref[...], mod_ref[...], sh_idx, sc_idx)
    h_hi, h_lo = _split_bf16(h)
    h_ref[...] = h_hi
    w_hi = wr_hi_ref[...]
    logits = jnp.dot(h_hi, w_hi, preferred_element_type=F32)
    logits += jnp.dot(h_lo, w_hi, preferred_element_type=F32)
    logits += jnp.dot(h_hi, wr_lo_ref[...], preferred_element_type=F32)
    r_ref[...] = _route(logits + br_ref[...])


def norm_mod(x, gains, layer, modseg, sh_idx, sc_idx, router=None):
    t, d = x.shape
    in_specs = [
        pl.BlockSpec((TM, d), lambda i: (i, 0)),
        pl.BlockSpec((None, 1, d), lambda i: (layer, 0, 0)),
        pl.BlockSpec((None, MOD_ROWS, d), lambda i: (i * TM // SEG, 0, 0)),
    ]
    h_spec = pl.BlockSpec((TM, d), lambda i: (i, 0))
    h_shape = jax.ShapeDtypeStruct((t, d), BF16)
    gains3 = gains.reshape(gains.shape[0], 1, d)
    if router is None:
        return pl.pallas_call(
            functools.partial(_norm_kernel, sh_idx=sh_idx, sc_idx=sc_idx),
            grid=(t // TM,), in_specs=in_specs, out_specs=h_spec, out_shape=h_shape,
            compiler_params=_cparams(1), name="norm_mod",
        )(x, gains3, modseg)
    wr_hi, wr_lo, br = router
    const = lambda i: (0, 0)
    return pl.pallas_call(
        functools.partial(_norm_route_kernel, sh_idx=sh_idx, sc_idx=sc_idx),
        grid=(t // TM,),
        in_specs=in_specs + [pl.BlockSpec((d, LANES), const), pl.BlockSpec((d, LANES), const),
                             pl.BlockSpec((1, LANES), const)],
        out_specs=[h_spec, pl.BlockSpec((TM, LANES), lambda i: (i, 0))],
        out_shape=[h_shape, jax.ShapeDtypeStruct((t, LANES), F32)],
        compiler_params=_cparams(1), name="norm_route",
    )(x, gains3, modseg, wr_hi, wr_lo, br)


def _final_norm_kernel(x_ref, gain_ref, o_ref):
    x = x_ref[...]
    ms = jnp.mean(x * x, axis=-1, keepdims=True)
    o_ref[...] = x * lax.rsqrt(ms + NORM_EPS) * gain_ref[...]


def rms_final(x, gain):
    t, d = x.shape
    return pl.pallas_call(
        _final_norm_kernel, grid=(t // TM,),
        in_specs=[pl.BlockSpec((TM, d), lambda i: (i, 0)), pl.BlockSpec((1, d), lambda i: (0, 0))],
        out_specs=pl.BlockSpec((TM, d), lambda i: (i, 0)),
        out_shape=jax.ShapeDtypeStruct((t, d), F32),
        compiler_params=_cparams(1), name="final_norm",
    )(x, gain.reshape(1, d))


def _mm_kernel(h_ref, w_ref, o_ref, wbf_ref):
    @pl.when(pl.program_id(1) == 0)
    def _():
        wbf_ref[...] = w_ref[...].astype(BF16)

    o_ref[...] = jnp.dot(h_ref[...], wbf_ref[...], preferred_element_type=F32).astype(o_ref.dtype)


def _mm_resid_kernel(h_ref, w_ref, x_ref, mod_ref, o_ref, wbf_ref, *, g_idx):
    @pl.when(pl.program_id(1) == 0)
    def _():
        wbf_ref[...] = w_ref[...].astype(BF16)

    acc = jnp.dot(h_ref[...], wbf_ref[...], preferred_element_type=F32)
    o_ref[...] = x_ref[...] + mod_ref[g_idx:g_idx + 1, :] * acc


def matmul(h, w, layer, col0, n_cols, out_dtype, row0=0, n_rows=None):
    k = h.shape[1]
    n_rows = h.shape[0] - row0 if n_rows is None else n_rows
    tn = min(TN, n_cols)
    r0, c0 = row0 // TM, col0 // tn
    return pl.pallas_call(
        _mm_kernel,
        grid=(n_cols // tn, n_rows // TM),
        in_specs=[pl.BlockSpec((TM, k), lambda j, i: (r0 + i, 0)),
                  pl.BlockSpec((None, k, tn), lambda j, i: (layer, 0, c0 + j))],
        out_specs=pl.BlockSpec((TM, tn), lambda j, i: (i, j)),
        out_shape=jax.ShapeDtypeStruct((n_rows, n_cols), out_dtype),
        scratch_shapes=[pltpu.VMEM((k, tn), BF16)],
        compiler_params=_cparams(2), name="matmul",
    )(h, w)


def matmul_resid(h, w, layer, x, modseg, g_idx):
    t, k = h.shape
    n = w.shape[-1]
    tn = min(TN, n)
    return pl.pallas_call(
        functools.partial(_mm_resid_kernel, g_idx=g_idx),
        grid=(n // tn, t // TM),
        in_specs=[pl.BlockSpec((TM, k), lambda j, i: (i, 0)),
                  pl.BlockSpec((None, k, tn), lambda j, i: (layer, 0, j)),
                  pl.BlockSpec((TM, tn), lambda j, i: (i, j)),
                  pl.BlockSpec((None, MOD_ROWS, tn), lambda j, i: (i * TM // SEG, 0, j))],
        out_specs=pl.BlockSpec((TM, tn), lambda j, i: (i, j)),
        out_shape=jax.ShapeDtypeStruct((t, n), F32),
        scratch_shapes=[pltpu.VMEM((k, tn), BF16)],
        compiler_params=_cparams(2), name="matmul_resid",
    )(h, w, x, modseg)


def _rope(x, cos, sin):
    lane = lax.broadcasted_iota(jnp.int32, x.shape, 1)
    first_half = (lane % DA_QK_DIM) < (DA_QK_DIM // 2)
    partner = jnp.where(first_half, pltpu.roll(x, LANES - DA_QK_DIM // 2, 1), pltpu.roll(x, DA_QK_DIM // 2, 1))
    return x * cos + partner * sin


def _softmax_parts(q_bf, kb, kcb):
    dn = (((1,), (1,)), ((), ()))
    s = lax.dot_general(q_bf, kb, dn, preferred_element_type=F32)
    m = jnp.max(s, axis=1, keepdims=True)
    if kcb is not None:
        sc = lax.dot_general(q_bf, kcb, dn, preferred_element_type=F32)
        m = jnp.maximum(m, jnp.max(sc, axis=1, keepdims=True))
    p = jnp.exp(s - m)
    l = jnp.sum(p, axis=1, keepdims=True)
    pc = None
    if kcb is not None:
        pc = jnp.exp(sc - m)
        l = l + jnp.sum(pc, axis=1, keepdims=True)
    return p, pc, l


def _attn_kernel(*refs, diff, rope, n_ctx, scale, out_scale):
    it = iter(refs)
    q_ref, k_ref, v_ref = next(it), next(it), next(it)
    ck_ref = cv_ref = None
    if n_ctx:
        ck_ref, cv_ref = next(it), next(it)
    if rope:
        cosq_ref, sinq_ref, cosk_ref, sink_ref = next(it), next(it), next(it), next(it)
    if diff:
        lam_ref, g_ref = next(it), next(it)
    o_ref = next(it)
    kb_ref, vb_ref = next(it), next(it)
    if n_ctx:
        kcb_ref, vcb_ref = next(it), next(it)

    @pl.when(pl.program_id(2) == 0)
    def _():
        k = k_ref[...].astype(F32)
        if rope:
            k = _rope(k, cosk_ref[...], sink_ref[...])
        kb_ref[...] = k.astype(BF16)
        vb_ref[...] = v_ref[...].astype(BF16)
        if n_ctx:
            kcb_ref[...] = ck_ref[...].astype(BF16)
            vcb_ref[...] = cv_ref[...].astype(BF16)

    q = q_ref[...].astype(F32)
    if rope:
        q = _rope(q, cosq_ref[...], sinq_ref[...])
    q = q * scale
    kb = kb_ref[...]
    kcb = kcb_ref[...] if n_ctx else None
    if diff:
        lane = lax.broadcasted_iota(jnp.int32, q.shape, 1)
        lam = lam_ref[0:1, 0:1]
        p0, pc0, l0 = _softmax_parts(jnp.where(lane < DA_QK_DIM, q, 0.0).astype(BF16), kb, kcb)
        p1, pc1, l1 = _softmax_parts(jnp.where(lane >= DA_QK_DIM, q, 0.0).astype(BF16), kb, kcb)
        w0 = 1.0 / l0
        w1 = lam / l1
        a = (p0 * w0 - p1 * w1).astype(BF16)
        o = jnp.dot(a, vb_ref[...], preferred_element_type=F32)
        if n_ctx:
            ac = (pc0 * w0 - pc1 * w1).astype(BF16)
            o += jnp.dot(ac, vcb_ref[...], preferred_element_type=F32)
        ms = jnp.mean(o * o, axis=-1, keepdims=True)
        o = (o * lax.rsqrt(ms + NORM_EPS) * g_ref[...]) * out_scale
    else:
        p, pc, l = _softmax_parts(q.astype(BF16), kb, kcb)
        w = 1.0 / l
        o = jnp.dot((p * w).astype(BF16), vb_ref[...], preferred_element_type=F32)
        if n_ctx:
            o += jnp.dot((pc * w).astype(BF16), vcb_ref[...], preferred_element_type=F32)
    o_ref[...] = o.astype(o_ref.dtype)


def attention(qkv, *, row0, n_batch, n_seq, tq, diff, ctx=None, rope_tabs=None, lam=None, subln=None,
              out_scale=1.0, scale):
    nq = n_seq // tq
    qb0, kb0 = row0 // tq, row0 // n_seq
    diff_ = diff
    rope = rope_tabs is not None
    n_ctx = 0 if ctx is None else ctx[0].shape[2]
    in_specs = [
        pl.BlockSpec((tq, LANES), lambda b, h, i: (qb0 + b * nq + i, h)),
        pl.BlockSpec((n_seq, LANES), lambda b, h, i: (kb0 + b, HEADS + h)),
        pl.BlockSpec((n_seq, LANES), lambda b, h, i: (kb0 + b, 2 * HEADS + h)),
    ]
    args = [qkv, qkv, qkv]
    scratch = [pltpu.VMEM((n_seq, LANES), BF16), pltpu.VMEM((n_seq, LANES), BF16)]
    if n_ctx:
        ck, cv, cl = ctx
        cspec = pl.BlockSpec((None, None, n_ctx, LANES), lambda b, h, i: (b, cl, 0, h))
        in_specs += [cspec, cspec]
        args += [ck, cv]
        scratch += [pltpu.VMEM((n_ctx, LANES), BF16), pltpu.VMEM((n_ctx, LANES), BF16)]
    if rope:
        cos, sin = rope_tabs
        in_specs += [pl.BlockSpec((tq, LANES), lambda b, h, i: (i, 0))] * 2
        in_specs += [pl.BlockSpec((n_seq, LANES), lambda b, h, i: (0, 0))] * 2
        args += [cos, sin, cos, sin]
    if diff_:
        in_specs += [pl.BlockSpec((1, LANES), lambda b, h, i: (0, 0))] * 2
        args += [lam, subln]
    return pl.pallas_call(
        functools.partial(_attn_kernel, diff=diff_, rope=rope, n_ctx=n_ctx, scale=scale, out_scale=out_scale),
        grid=(n_batch, HEADS, nq),
        in_specs=in_specs,
        out_specs=pl.BlockSpec((tq, LANES), lambda b, h, i: (b * nq + i, h)),
        out_shape=jax.ShapeDtypeStruct((n_batch * n_seq, D_MODEL), BF16),
        scratch_shapes=scratch,
        compiler_params=_cparams(3), name="attention",
    )(*args)


def _na_key_row0(t):
    return jnp.clip(t * NA_TILE_ROWS - NA_WIN_ROWS // 2, 0, DEC_SEQ // GRID_W - NA_KEY_ROWS)


def _na_kernel(q_ref, k_ref, v_ref, ck_ref, cv_ref, bias_ref, o_ref, kb_ref, vb_ref, kcb_ref, vcb_ref, *, scale):
    t = pl.program_id(2)

    @pl.when(t == 0)
    def _():
        kb_ref[...] = k_ref[...].astype(BF16)
        vb_ref[...] = v_ref[...].astype(BF16)
        kcb_ref[...] = ck_ref[...].astype(BF16)
        vcb_ref[...] = cv_ref[...].astype(BF16)

    n_keys = NA_KEY_ROWS * GRID_W
    k0 = pl.multiple_of(_na_key_row0(t) * GRID_W, GRID_W)
    kw = kb_ref[pl.ds(k0, n_keys), :]
    vw = vb_ref[pl.ds(k0, n_keys), :]
    q = (q_ref[...].astype(F32) * scale).astype(BF16)
    dn = (((1,), (1,)), ((), ()))
    s = lax.dot_general(q, kw, dn, preferred_element_type=F32) + bias_ref[...]
    sc = lax.dot_general(q, kcb_ref[...], dn, preferred_element_type=F32)
    m = jnp.maximum(jnp.max(s, axis=1, keepdims=True), jnp.max(sc, axis=1, keepdims=True))
    p = jnp.exp(s - m)
    pc = jnp.exp(sc - m)
    w = 1.0 / (jnp.sum(p, axis=1, keepdims=True) + jnp.sum(pc, axis=1, keepdims=True))
    o = jnp.dot((p * w).astype(BF16), vw, preferred_element_type=F32)
    o += jnp.dot((pc * w).astype(BF16), vcb_ref[...], preferred_element_type=F32)
    o_ref[...] = o.astype(o_ref.dtype)


def na_bias_table(rpb):
    rows = DEC_SEQ // GRID_W
    tiles = jnp.array([0, 1, rows // NA_TILE_ROWS - 1])
    j = jnp.arange(NA_TILE_ROWS)
    r = tiles[:, None] * NA_TILE_ROWS + j[None, :]
    r0 = jnp.clip(r - NA_WIN_ROWS // 2, 0, rows - NA_WIN_ROWS)
    kr = _na_key_row0(tiles)[:, None] + jnp.arange(NA_KEY_ROWS)[None, :]
    row_ok = (kr[:, None, :] >= r0[:, :, None]) & (kr[:, None, :] < r0[:, :, None] + NA_WIN_ROWS)
    dr_idx = jnp.clip(kr[:, None, :] - r[:, :, None] + NA_WIN_ROWS - 1, 0, 2 * NA_WIN_ROWS - 2)
    col = jnp.arange(GRID_W)
    col0 = jnp.clip(col - NA_WIN_COLS // 2, 0, GRID_W - NA_WIN_COLS)
    col_ok = (col[None, :] >= col0[:, None]) & (col[None, :] < col0[:, None] + NA_WIN_COLS)
    dc_idx = jnp.clip(col[None, :] - col[:, None], -(NA_WIN_COLS - 1), NA_WIN_COLS - 1) + NA_WIN_COLS - 1
    b = rpb.astype(F32)[:, dr_idx[:, :, None, :, None], dc_idx[None, None, :, None, :]]
    ok = row_ok[:, :, None, :, None] & col_ok[None, None, :, None, :]
    b = jnp.where(ok[None], b, -jnp.inf)
    return b.reshape(rpb.shape[0], 3, NA_TILE_ROWS * GRID_W, NA_KEY_ROWS * GRID_W)


def na_attention(qkv, row0, cache_k, cache_v, cl, bias, scale):
    tq = NA_TILE_ROWS * GRID_W
    nt = DEC_SEQ // tq
    qb0, kb0 = row0 // tq, row0 // DEC_SEQ
    n_ctx = cache_k.shape[2]
    cspec = pl.BlockSpec((None, None, n_ctx, LANES), lambda b, h, t: (b, cl, 0, h))
    seq_bf = pltpu.VMEM((DEC_SEQ, LANES), BF16)
    ctx_bf = pltpu.VMEM((n_ctx, LANES), BF16)
    return pl.pallas_call(
        functools.partial(_na_kernel, scale=scale),
        grid=(DEC_BATCH, HEADS, nt),
        in_specs=[
            pl.BlockSpec((tq, LANES), lambda b, h, t: (qb0 + b * nt + t, h)),
            pl.BlockSpec((DEC_SEQ, LANES), lambda b, h, t: (kb0 + b, HEADS + h)),
            pl.BlockSpec((DEC_SEQ, LANES), lambda b, h, t: (kb0 + b, 2 * HEADS + h)),
            cspec, cspec,
            pl.BlockSpec((None, None, tq, NA_KEY_ROWS * GRID_W),
                         lambda b, h, t: (h, jnp.minimum(t, 1) + t // (nt - 1), 0, 0)),
        ],
        out_specs=pl.BlockSpec((tq, LANES), lambda b, h, t: (b * nt + t, h)),
        out_shape=jax.ShapeDtypeStruct((T_SAMPLE, D_MODEL), BF16),
        scratch_shapes=[seq_bf, seq_bf, ctx_bf, ctx_bf],
        compiler_params=_cparams(3), name="na_attention",
    )(qkv, qkv, qkv, cache_k, cache_v, bias)


def _block_edge(x, m, rev, row):
    n = x.shape[0]
    if m == 1:
        return x
    if m >= 8:
        x3 = x.reshape(n // m, m, x.shape[1])
        edge = x3[:, 0:1, :] if rev else x3[:, m - 1:m, :]
        return jnp.broadcast_to(edge, x3.shape).reshape(x.shape)
    y = x
    for bit in range(m.bit_length() - 1):
        step = 1 << bit
        has_bit = ((row >> bit) & 1) == 1
        if rev:
            y = jnp.where(has_bit, pltpu.roll(y, step, 0), y)
        else:
            y = jnp.where(has_bit, y, pltpu.roll(y, n - step, 0))
    return y


def _hgrn_tile(q, kk, logf, v_f32, st_bf, rev):
    c = q.shape[0]
    row = lax.broadcasted_iota(jnp.int32, (c, c), 0)
    col = lax.broadcasted_iota(jnp.int32, (c, c), 1)
    dn = (((1,), (1,)), ((), ()))
    x = logf
    att = jnp.zeros((c, c), F32)
    for lvl in range(c.bit_length() - 1):
        m = 1 << lvl
        tot = _block_edge(x, m, rev, row)
        upper = ((row >> lvl) & 1) == 1
        is_q = jnp.logical_not(upper) if rev else upper
        qp = jnp.where(is_q, q * jnp.exp(x), 0.0).astype(BF16)
        kp = jnp.where(is_q, 0.0, kk * jnp.exp(tot - x)).astype(BF16)
        prod = lax.dot_general(qp, kp, dn, preferred_element_type=F32)
        att = att + jnp.where((row >> (lvl + 1)) == (col >> (lvl + 1)), prod, 0.0)
        sib = pltpu.roll(tot, (c - m) if rev else m, 0)
        x = x + jnp.where(is_q, sib, 0.0)
    tot = x[0:1, :] if rev else x[c - 1:c, :]
    v_bf = v_f32.astype(BF16)
    diag = jnp.sum(q * kk, axis=1, keepdims=True)
    o = jnp.dot(att.astype(BF16), v_bf, preferred_element_type=F32) + diag * v_f32
    o += lax.dot_general((q * jnp.exp(x)).astype(BF16), st_bf, dn, preferred_element_type=F32)
    k_out = (kk * jnp.exp(tot - x)).astype(BF16)
    inc = jnp.dot(v_f32.T.astype(BF16), k_out, preferred_element_type=F32)
    return o, inc, tot


def _hgrn_kernel(*refs, n, has_s0, emit_state):
    it = iter(refs)
    q_ref, i_ref, zf_ref, zb_ref, g_ref = next(it), next(it), next(it), next(it), next(it)
    s0_ref = next(it) if has_s0 else None
    lb_ref, gn_ref = next(it), next(it)
    o_ref = next(it)
    sfin_ref = next(it) if emit_state else None
    of_ref, st_ref = next(it), next(it)
    nt = n // HG_TILE
    log_lb, log1m_lb, one_m_lb = lb_ref[0:1, :], lb_ref[1:2, :], lb_ref[2:3, :]

    def gates(z):
        e = jnp.exp(-jnp.abs(z))
        log_sig = jnp.minimum(z, 0.0) - jnp.log1p(e)
        b = log1m_lb + log_sig
        logf = jnp.maximum(log_lb, b) + jnp.log1p(jnp.exp(-jnp.abs(log_lb - b)))
        kk = one_m_lb * (jnp.where(z > 0, e, 1.0) / (1.0 + e))
        return logf, kk

    def load_q(rows):
        qr = q_ref[rows, :].astype(F32)
        return (qr / (1.0 + jnp.exp(-qr))) * (HEAD_DIM ** -0.5)

    for d, z_ref in enumerate((zf_ref, zb_ref)):
        rev = d == 1
        if has_s0:
            st_ref[...] = s0_ref[d].T
        else:
            st_ref[...] = jnp.zeros_like(st_ref)

        def body(step, carry, rev=rev, z_ref=z_ref):
            ti = (nt - 1 - step) if rev else step
            rows = pl.ds(pl.multiple_of(ti * HG_TILE, HG_TILE), HG_TILE)
            logf, kk = gates(z_ref[rows, :])
            st = st_ref[...]
            o, inc, tot = _hgrn_tile(load_q(rows), kk, logf, i_ref[rows, :].astype(F32), st.astype(BF16), rev)
            st_ref[...] = st * jnp.exp(tot) + inc
            if not rev:
                of_ref[rows, :] = o
            else:
                o = o + of_ref[rows, :]
                ms = jnp.mean(o * o, axis=-1, keepdims=True)
                o = o * lax.rsqrt(ms + NORM_EPS) * gn_ref[...]
                g = g_ref[rows, :].astype(F32)
                o_ref[rows, :] = (o * (g / (1.0 + jnp.exp(-g)))).astype(o_ref.dtype)
            return carry

        lax.fori_loop(0, nt, body, 0)
        if emit_state:
            sfin_ref[d] = st_ref[...].T


def hgrn_scan(qi, z, g, lbp, layer, gnorm, *, row0, n_batch, n_seq, s0=None, s0_layer=0, emit_state=False):
    rb0 = row0 // n_seq
    blk = lambda c0: pl.BlockSpec((n_seq, LANES), lambda b, h: (rb0 + b, c0 + h))
    in_specs = [blk(0), blk(HEADS), blk(0), blk(HEADS), blk(0)]
    args = [qi, qi, z, z, g]
    if s0 is not None:
        in_specs.append(pl.BlockSpec((None, None, 2, None, HEAD_DIM, HEAD_DIM),
                                     lambda b, h: (b, s0_layer, 0, h, 0, 0)))
        args.append(s0)
    in_specs += [pl.BlockSpec((None, MOD_ROWS, LANES), lambda b, h: (layer, 0, h)),
                 pl.BlockSpec((1, LANES), lambda b, h: (0, 0))]
    args += [lbp, gnorm]
    out_specs = [pl.BlockSpec((n_seq, LANES), lambda b, h: (b, h))]
    out_shape = [jax.ShapeDtypeStruct((n_batch * n_seq, D_MODEL), BF16)]
    if emit_state:
        out_specs.append(pl.BlockSpec((None, 2, None, HEAD_DIM, HEAD_DIM), lambda b, h: (b, 0, h, 0, 0)))
        out_shape.append(jax.ShapeDtypeStruct((n_batch, 2, HEADS, HEAD_DIM, HEAD_DIM), F32))
    res = pl.pallas_call(
        functools.partial(_hgrn_kernel, n=n_seq, has_s0=s0 is not None, emit_state=emit_state),
        grid=(n_batch, HEADS),
        in_specs=in_specs, out_specs=out_specs, out_shape=out_shape,
        scratch_shapes=[pltpu.VMEM((n_seq, LANES), F32), pltpu.VMEM((HEAD_DIM, HEAD_DIM), F32)],
        compiler_params=_cparams(2), name="hgrn_scan",
    )(*args)
    return res if emit_state else res[0]


def _expert_kernel(be_ref, first_ref, nused_ref, x_ref, wg_ref, wu_ref, wd_ref, o_ref, wg_bf, wu_bf, wd_bf):
    g = pl.program_id(0)

    @pl.when(g < nused_ref[0])
    def _():
        @pl.when(first_ref[g] == 1)
        def _():
            wg_bf[...] = wg_ref[...].astype(BF16)
            wu_bf[...] = wu_ref[...].astype(BF16)
            wd_bf[...] = wd_ref[...].astype(BF16)

        x = x_ref[...]
        a = jnp.dot(x, wg_bf[...], preferred_element_type=F32)
        u = jnp.dot(x, wu_bf[...], preferred_element_type=F32)
        hmid = (a / (1.0 + jnp.exp(-a))) * u
        o_ref[...] = jnp.dot(hmid.astype(BF16), wd_bf[...], preferred_element_type=F32).astype(o_ref.dtype)

    @pl.when(g >= nused_ref[0])
    def _():
        o_ref[...] = jnp.zeros_like(o_ref)


def experts(xb, block_e, first, n_used, w_gate, w_up, w_down, layer):
    cap, d = xb.shape
    nb = cap // MOE_ROWS
    de = w_gate.shape[-1]
    grid_spec = pltpu.PrefetchScalarGridSpec(
        num_scalar_prefetch=3,
        grid=(nb,),
        in_specs=[
            pl.BlockSpec((MOE_ROWS, d), lambda g, be, fi, nu: (g, 0)),
            pl.BlockSpec((None, None, d, de), lambda g, be, fi, nu: (layer, be[g], 0, 0)),
            pl.BlockSpec((None, None, d, de), lambda g, be, fi, nu: (layer, be[g], 0, 0)),
            pl.BlockSpec((None, None, de, d), lambda g, be, fi, nu: (layer, be[g], 0, 0)),
        ],
        out_specs=pl.BlockSpec((MOE_ROWS, d), lambda g, be, fi, nu: (g, 0)),
        scratch_shapes=[pltpu.VMEM((d, de), BF16), pltpu.VMEM((d, de), BF16), pltpu.VMEM((de, d), BF16)],
    )
    return pl.pallas_call(
        _expert_kernel, grid_spec=grid_spec,
        out_shape=jax.ShapeDtypeStruct((cap, d), F32),
        compiler_params=_cparams(1), name="experts",
    )(block_e, first, n_used, xb, w_gate, w_up, w_down)


def _combine_kernel(x_ref, y_ref, r_ref, mod_ref, o_ref, *, g_idx):
    r = r_ref[...]
    d = x_ref.shape[1]
    y = r[:, 0:1] * y_ref[:, 0:d] + r[:, 1:2] * y_ref[:, d:2 * d]
    o_ref[...] = x_ref[...] + mod_ref[g_idx:g_idx + 1, :] * y


def combine(x, y2, route, modseg, g_idx):
    t, d = x.shape
    tm = 256
    return pl.pallas_call(
        functools.partial(_combine_kernel, g_idx=g_idx),
        grid=(t // tm,),
        in_specs=[pl.BlockSpec((tm, d), lambda i: (i, 0)),
                  pl.BlockSpec((tm, 2 * d), lambda i: (i, 0)),
                  pl.BlockSpec((tm, LANES), lambda i: (i, 0)),
                  pl.BlockSpec((None, MOD_ROWS, d), lambda i: (i * tm // SEG, 0, 0))],
        out_specs=pl.BlockSpec((tm, d), lambda i: (i, 0)),
        out_shape=jax.ShapeDtypeStruct((t, d), F32),
        compiler_params=_cparams(1), name="combine",
    )(x, y2, route, modseg)


def moe_dispatch_plan(e_idx):
    t, k = e_idx.shape
    n_assign = t * k
    flat_e = e_idx.reshape(-1)
    order = jnp.argsort(flat_e)
    sorted_e = flat_e[order]
    counts = jnp.bincount(flat_e, length=MOE_EXPERTS)
    padded = (counts + MOE_ROWS - 1) // MOE_ROWS * MOE_ROWS
    seg_end = jnp.cumsum(padded)
    seg_start = seg_end - padded
    raw_start = jnp.cumsum(counts) - counts
    dest = (seg_start[sorted_e] + jnp.arange(n_assign, dtype=jnp.int32) - raw_start[sorted_e]).astype(jnp.int32)
    n_blocks = -(-(n_assign + MOE_EXPERTS * (MOE_ROWS - 1)) // MOE_ROWS)
    cap = n_blocks * MOE_ROWS
    slot_tok = jnp.zeros((cap,), jnp.int32).at[dest].set((order // k).astype(jnp.int32))
    pos = jnp.zeros((n_assign,), jnp.int32).at[order].set(dest).reshape(t, k)
    block_e = jnp.minimum(jnp.searchsorted(seg_end, jnp.arange(n_blocks) * MOE_ROWS, side='right'),
                          MOE_EXPERTS - 1).astype(jnp.int32)
    first = jnp.concatenate([jnp.ones((1,), jnp.int32), (block_e[1:] != block_e[:-1]).astype(jnp.int32)])
    n_used = (seg_end[-1] // MOE_ROWS).astype(jnp.int32).reshape(1)
    return slot_tok, pos, block_e, first, n_used


def _rope_tables(n, dim):
    quarter = dim // 4
    inv = jnp.power(ROPE_BASE, -jnp.arange(quarter, dtype=F32) / quarter)
    t = jnp.arange(n)
    row = (t // GRID_W).astype(F32)
    col = (t % GRID_W).astype(F32)
    ang = jnp.concatenate([row[:, None] * inv, col[:, None] * inv], axis=-1)
    cos, sin = jnp.cos(ang), jnp.sin(ang)
    reps = LANES // dim
    cos_l = jnp.tile(jnp.concatenate([cos, cos], axis=-1), (1, reps))
    sin_l = jnp.tile(jnp.concatenate([-sin, sin], axis=-1), (1, reps))
    return cos_l, sin_l


def kernel(x_prompt, x_sample, state_hgrn, cache_diff_k, cache_diff_v, cache_na_k, cache_na_v, c, c_ctx, norm_mix, norm_ffn, w_ada, b_ada, router_group_w, router_group_b, router_expert_w, router_expert_b, expert_w_gate, expert_w_up, expert_w_down, hg_w_in, hg_lb_logits, hg_gnorm, hg_w_out, da_w_qkv, da_lambda, da_subln, da_w_out, na_w_qkv, na_rpb, na_w_out, final_norm):
    d = D_MODEL
    x = jnp.concatenate([x_prompt.reshape(T_PROMPT, d), x_sample.reshape(T_SAMPLE, d)], axis=0)

    cond8 = jnp.zeros((MOD_ROWS, d), F32).at[0].set(c_ctx).at[1:1 + DEC_BATCH].set(c)
    mod_all = ada_all(cond8, w_ada, b_ada)
    seg_cond = jnp.array([0] * (T_PROMPT // SEG) + list(range(1, 1 + DEC_BATCH)), jnp.int32)
    mod_all = mod_all.reshape(DEPTH, MOD_ROWS, N_MOD, d)[:, seg_cond]
    mod_all = jnp.pad(mod_all, ((0, 0), (0, 0), (0, MOD_ROWS - N_MOD), (0, 0)))

    p = jax.nn.softmax(hg_lb_logits.astype(F32), axis=0)
    cum = jnp.cumsum(p, axis=0)
    lb = cum - cum[0:1]
    lbp = jnp.stack([jnp.log(lb), jnp.log1p(-lb), 1.0 - lb], axis=1)
    lbp = jnp.pad(lbp, ((0, 0), (0, MOD_ROWS - 3), (0, 0)))

    wr = jnp.concatenate([router_group_w, router_expert_w], axis=-1)
    wr = jnp.pad(wr, ((0, 0), (0, 0), (0, LANES - wr.shape[-1])))
    wr_hi = wr.astype(BF16)
    wr_lo = (wr - wr_hi.astype(F32)).astype(BF16)
    br = jnp.concatenate([router_group_b, router_expert_b], axis=-1).astype(F32)
    br = jnp.pad(br, ((0, 0), (0, LANES - br.shape[-1])))[:, None, :]

    cos_l, sin_l = _rope_tables(DEC_SEQ, DA_QK_DIM)
    cdk = cache_diff_k.reshape(DEC_BATCH, -1, PAST_LEN, d)
    cdv = cache_diff_v.reshape(DEC_BATCH, -1, PAST_LEN, d)
    cnk = cache_na_k.reshape(DEC_BATCH, -1, PAST_LEN, d)
    cnv = cache_na_v.reshape(DEC_BATCH, -1, PAST_LEN, d)

    new_hg, new_dk, new_dv, new_nk, new_nv = [], [], [], [], []
    for i in range(DEPTH):
        kind, j = i % N_MIXERS, i // N_MIXERS
        modseg = mod_all[i]
        h = norm_mod(x, norm_mix, i, modseg, 0, 1)
        if kind == 0:
            qi = matmul(h, hg_w_in, j, 0, 2 * d, BF16)
            z = matmul(h, hg_w_in, j, 2 * d, 2 * d, F32)
            g = matmul(h, hg_w_in, j, 4 * d, d, BF16)
            gn = hg_gnorm[j].reshape(1, HEAD_DIM)
            op, s_fin = hgrn_scan(qi, z, g, lbp, i, gn, row0=0, n_batch=BATCH, n_seq=SEQ, emit_state=True)
            os_ = hgrn_scan(qi, z, g, lbp, i, gn, row0=T_PROMPT, n_batch=DEC_BATCH, n_seq=DEC_SEQ,
                            s0=state_hgrn, s0_layer=j)
            new_hg.append(s_fin)
            o = jnp.concatenate([op, os_], axis=0)
            w_out = hg_w_out
        elif kind == 1:
            lam_init = 0.8 - 0.6 * math.exp(-0.3 * i)
            lp = da_lambda[j].astype(F32)
            lam = jnp.exp(jnp.sum(lp[0] * lp[1])) - jnp.exp(jnp.sum(lp[2] * lp[3])) + lam_init
            lam_row = jnp.full((1, LANES), lam, F32)
            subln = da_subln[j].reshape(1, HEAD_DIM).astype(F32)
            qkv = matmul(h, da_w_qkv, j, 0, 3 * d, F32)
            scale = DA_QK_DIM ** -0.5
            op = attention(qkv, row0=0, n_batch=BATCH, n_seq=SEQ, tq=SEQ, diff=True, lam=lam_row, subln=subln,
                           out_scale=1.0 - lam_init, scale=scale)
            os_ = attention(qkv, row0=T_PROMPT, n_batch=DEC_BATCH, n_seq=DEC_SEQ, tq=256, diff=True,
                            ctx=(cdk, cdv, j), rope_tabs=(cos_l, sin_l), lam=lam_row, subln=subln,
                            out_scale=1.0 - lam_init, scale=scale)
            new_dk.append(qkv[:T_PROMPT, d:2 * d].reshape(BATCH, SEQ, HEADS, 2, DA_QK_DIM))
            new_dv.append(qkv[:T_PROMPT, 2 * d:].reshape(BATCH, SEQ, HEADS, HEAD_DIM))
            o = jnp.concatenate([op, os_], axis=0)
            w_out = da_w_out
        else:
            qkv = matmul(h, na_w_qkv, j, 0, 3 * d, F32)
            scale = HEAD_DIM ** -0.5
            op = attention(qkv, row0=0, n_batch=BATCH, n_seq=SEQ, tq=SEQ, diff=False, scale=scale)
            os_ = na_attention(qkv, T_PROMPT, cnk, cnv, j, na_bias_table(na_rpb[j]), scale)
            new_nk.append(qkv[:T_PROMPT, d:2 * d].reshape(BATCH, SEQ, HEADS, HEAD_DIM))
            new_nv.append(qkv[:T_PROMPT, 2 * d:].reshape(BATCH, SEQ, HEADS, HEAD_DIM))
            o = jnp.concatenate([op, os_], axis=0)
            w_out = na_w_out
        x = matmul_resid(o, w_out, j, x, modseg, 2)

        h, route = norm_mod(x, norm_ffn, i, modseg, 3, 4, router=(wr_hi[i], wr_lo[i], br[i]))
        e_idx = route[:, 2:4].astype(jnp.int32)
        slot_tok, pos, block_e, first, n_used = moe_dispatch_plan(e_idx)
        xb = jnp.take(h, slot_tok, axis=0)
        yb = experts(xb, block_e, first, n_used, expert_w_gate, expert_w_up, expert_w_down, i)
        y2 = jnp.take(yb, pos.reshape(-1), axis=0).reshape(T_ALL, 2 * d)
        x = combine(x, y2, route, modseg, 5)

    y = rms_final(x, final_norm)
    return (y[:T_PROMPT].reshape(BATCH, SEQ, d), y[T_PROMPT:].reshape(DEC_BATCH, DEC_SEQ, d),
            jnp.stack(new_hg, axis=1), jnp.stack(new_dk, axis=1), jnp.stack(new_dv, axis=1),
            jnp.stack(new_nk, axis=1), jnp.stack(new_nv, axis=1))
```

```python
import functools
import math

import jax
import jax.numpy as jnp
from jax import lax
from jax.experimental import pallas as pl
from jax.experimental.pallas import tpu as pltpu

D_MODEL = 2048
BATCH = 32
SEQ = 256
DEPTH = 4
DEC_BATCH = 4
DEC_SEQ = 4096
PAST_LEN = 512
GRID_W = 64
N_MIXERS = 3
HEADS = 16
HEAD_DIM = D_MODEL // HEADS
DA_QK_DIM = 64
NA_WIN_ROWS = 8
NA_WIN_COLS = 16
MOE_GROUPS = 4
MOE_EXPERTS_PER_GROUP = 8
MOE_EXPERTS = MOE_GROUPS * MOE_EXPERTS_PER_GROUP
MOE_D_EXPERT = 512
ROPE_BASE = 10000.0
NORM_EPS = 1e-6
N_MOD = 6

T_PROMPT = BATCH * SEQ
T_SAMPLE = DEC_BATCH * DEC_SEQ
T_ALL = T_PROMPT + T_SAMPLE
SEG = DEC_SEQ
N_SEG = T_ALL // SEG
MOD_ROWS = 8

LANES = 128
VMEM_LIMIT = 56 * 1024 * 1024

TM = 512
TN = 1024
MOE_ROWS = 256
HG_TILE = 128
NA_TILE_ROWS = 8
NA_KEY_ROWS = 16

F32 = jnp.float32
BF16 = jnp.bfloat16


def _cparams(n_axes):
    return pltpu.CompilerParams(dimension_semantics=("arbitrary",) * n_axes, vmem_limit_bytes=VMEM_LIMIT)


def _split_bf16(x):
    hi = x.astype(BF16)
    lo = (x - hi.astype(F32)).astype(BF16)
    return hi, lo


def _ada_kernel(cond_ref, w_ref, b_ref, o_ref):
    a = cond_ref[...]
    a = a / (1.0 + jnp.exp(-a))
    a_hi, a_lo = _split_bf16(a)
    w_hi, w_lo = _split_bf16(w_ref[...])
    acc = jnp.dot(a_hi, w_hi, preferred_element_type=F32)
    acc += jnp.dot(a_lo, w_hi, preferred_element_type=F32)
    acc += jnp.dot(a_hi, w_lo, preferred_element_type=F32)
    o_ref[...] = acc + b_ref[...]


def ada_all(cond8, w_ada, b_ada):
    depth, d, n = w_ada.shape
    tn = 1024
    return pl.pallas_call(
        _ada_kernel,
        grid=(depth, n // tn),
        in_specs=[
            pl.BlockSpec((MOD_ROWS, d), lambda l, j: (0, 0)),
            pl.BlockSpec((None, d, tn), lambda l, j: (l, 0, j)),
            pl.BlockSpec((None, 1, tn), lambda l, j: (l, 0, j)),
        ],
        out_specs=pl.BlockSpec((None, MOD_ROWS, tn), lambda l, j: (l, 0, j)),
        out_shape=jax.ShapeDtypeStruct((depth, MOD_ROWS, n), F32),
        compiler_params=_cparams(2),
        name="ada",
    )(cond8, w_ada, b_ada.reshape(depth, 1, n))


def _norm_mod(x, gain, mod, sh_idx, sc_idx):
    ms = jnp.mean(x * x, axis=-1, keepdims=True)
    y = x * lax.rsqrt(ms + NORM_EPS) * gain
    return y * (1.0 + mod[sc_idx:sc_idx + 1, :]) + mod[sh_idx:sh_idx + 1, :]


def _norm_kernel(x_ref, gain_ref, mod_ref, h_ref, *, sh_idx, sc_idx):
    h_ref[...] = _norm_mod(x_ref[...], gain_ref[...], mod_ref[...], sh_idx, sc_idx).astype(BF16)


def _route(logits):
    lane = lax.broadcasted_iota(jnp.int32, logits.shape, 1)
    neg = jnp.float32(-jnp.inf)
    big = jnp.int32(LANES)
    lg = jnp.where(lane < MOE_GROUPS, logits, neg)
    mg = jnp.max(lg, axis=1, keepdims=True)
    grp = jnp.min(jnp.where(lg == mg, lane, big), axis=1, keepdims=True)
    p_grp = 1.0 / jnp.sum(jnp.exp(lg - mg), axis=1, keepdims=True)
    lo = MOE_GROUPS + grp * MOE_EXPERTS_PER_GROUP
    le = jnp.where((lane >= lo) & (lane < lo + MOE_EXPERTS_PER_GROUP), logits, neg)
    m1 = jnp.max(le, axis=1, keepdims=True)
    i1 = jnp.min(jnp.where(le == m1, lane, big), axis=1, keepdims=True)
    le2 = jnp.where(lane == i1, neg, le)
    m2 = jnp.max(le2, axis=1, keepdims=True)
    i2 = jnp.min(jnp.where(le2 == m2, lane, big), axis=1, keepdims=True)
    p2 = jnp.exp(m2 - m1)
    g1 = p_grp / (1.0 + p2)
    g2 = p_grp * p2 / (1.0 + p2)
    return g1, g2, i1 - MOE_GROUPS, i2 - MOE_GROUPS


def _norm_route_kernel(x_ref, gain_ref, mod_ref, wr_hi_ref, wr_lo_ref, br_ref, h_ref, r_ref, cnt_ref, run_ref,
                       *, sh_idx, sc_idx):
    @pl.when(pl.program_id(0) == 0)
    def _():
        run_ref[...] = jnp.zeros_like(run_ref)

    h = _norm_mod(x_ref[...], gain_ref[...], mod_ref[...], sh_idx, sc_idx)
    h_ref[...] = h
    h_hi, h_lo = _split_bf16(h)
    w_hi = wr_hi_ref[...]
    logits = jnp.dot(h_hi, w_hi, preferred_element_type=F32)
    logits += jnp.dot(h_lo, w_hi, preferred_element_type=F32)
    logits += jnp.dot(h_hi, wr_lo_ref[...], preferred_element_type=F32)
    g1, g2, e1, e2 = _route(logits + br_ref[...])

    tm = h.shape[0]
    lane = lax.broadcasted_iota(jnp.int32, (tm, LANES), 1)
    o1 = (lane == e1).astype(F32)
    o2 = (lane == e2).astype(F32)
    before = (lax.broadcasted_iota(jnp.int32, (tm, tm), 0) > lax.broadcasted_iota(jnp.int32, (tm, tm), 1)).astype(BF16)
    p1 = jnp.dot(before, o1.astype(BF16), preferred_element_type=F32)
    p2 = jnp.dot(before, o2.astype(BF16), preferred_element_type=F32)
    tot1 = jnp.sum(o1, axis=0, keepdims=True)
    tot2 = jnp.sum(o2, axis=0, keepdims=True)
    run = run_ref[0:1, :]
    rank1 = jnp.sum(o1 * (run + p1), axis=1, keepdims=True)
    rank2 = jnp.sum(o2 * (run + tot1 + p2), axis=1, keepdims=True)
    run = run + tot1 + tot2
    run_ref[0:1, :] = run
    cnt_ref[...] = jnp.broadcast_to(run, cnt_ref.shape)

    out = jnp.where(lane == 0, g1, 0.0)
    out = jnp.where(lane == 1, g2, out)
    out = jnp.where(lane == 2, e1.astype(F32), out)
    out = jnp.where(lane == 3, e2.astype(F32), out)
    out = jnp.where(lane == 4, rank1, out)
    out = jnp.where(lane == 5, rank2, out)
    r_ref[...] = out


def norm_mod(x, gains, layer, modseg, sh_idx, sc_idx, router=None):
    t, d = x.shape
    in_specs = [
        pl.BlockSpec((TM, d), lambda i: (i, 0)),
        pl.BlockSpec((None, 1, d), lambda i: (layer, 0, 0)),
        pl.BlockSpec((None, MOD_ROWS, d), lambda i: (i * TM // SEG, 0, 0)),
    ]
    h_spec = pl.BlockSpec((TM, d), lambda i: (i, 0))
    h_shape = jax.ShapeDtypeStruct((t, d), BF16)
    gains3 = gains.reshape(gains.shape[0], 1, d)
    if router is None:
        return pl.pallas_call(
            functools.partial(_norm_kernel, sh_idx=sh_idx, sc_idx=sc_idx),
            grid=(t // TM,), in_specs=in_specs, out_specs=h_spec, out_shape=h_shape,
            compiler_params=_cparams(1), name="norm_mod",
        )(x, gains3, modseg)
    wr_hi, wr_lo, br = router
    const = lambda i: (0, 0)
    return pl.pallas_call(
        functools.partial(_norm_route_kernel, sh_idx=sh_idx, sc_idx=sc_idx),
        grid=(t // TM,),
        in_specs=in_specs + [pl.BlockSpec((d, LANES), const), pl.BlockSpec((d, LANES), const),
                             pl.BlockSpec((1, LANES), const)],
        out_specs=[h_spec, pl.BlockSpec((TM, LANES), lambda i: (i, 0)), pl.BlockSpec((MOD_ROWS, LANES), const)],
        out_shape=[jax.ShapeDtypeStruct((t, d), F32), jax.ShapeDtypeStruct((t, LANES), F32),
                   jax.ShapeDtypeStruct((MOD_ROWS, LANES), F32)],
        scratch_shapes=[pltpu.VMEM((MOD_ROWS, LANES), F32)],
        compiler_params=_cparams(1), name="norm_route",
    )(x, gains3, modseg, wr_hi, wr_lo, br)


def _mm_kernel(h_ref, w_ref, o_ref, wbf_ref):
    @pl.when(pl.program_id(1) == 0)
    def _():
        wbf_ref[...] = w_ref[...].astype(BF16)

    o_ref[...] = jnp.dot(h_ref[...], wbf_ref[...], preferred_element_type=F32).astype(o_ref.dtype)


def _mm_resid_kernel(h_ref, w_ref, x_ref, mod_ref, o_ref, wbf_ref, *, g_idx):
    @pl.when(pl.program_id(1) == 0)
    def _():
        wbf_ref[...] = w_ref[...].astype(BF16)

    acc = jnp.dot(h_ref[...], wbf_ref[...], preferred_element_type=F32)
    o_ref[...] = x_ref[...] + mod_ref[g_idx:g_idx + 1, :] * acc


def matmul(h, w, layer, col0, n_cols, out_dtype, row0=0, n_rows=None):
    k = h.shape[1]
    n_rows = h.shape[0] - row0 if n_rows is None else n_rows
    tn = min(TN, n_cols)
    r0, c0 = row0 // TM, col0 // tn
    return pl.pallas_call(
        _mm_kernel,
        grid=(n_cols // tn, n_rows // TM),
        in_specs=[pl.BlockSpec((TM, k), lambda j, i: (r0 + i, 0)),
                  pl.BlockSpec((None, k, tn), lambda j, i: (layer, 0, c0 + j))],
        out_specs=pl.BlockSpec((TM, tn), lambda j, i: (i, j)),
        out_shape=jax.ShapeDtypeStruct((n_rows, n_cols), out_dtype),
        scratch_shapes=[pltpu.VMEM((k, tn), BF16)],
        compiler_params=_cparams(2), name="matmul",
    )(h, w)


def matmul_resid(h, w, layer, x, modseg, g_idx):
    t, k = h.shape
    n = w.shape[-1]
    tn = min(TN, n)
    return pl.pallas_call(
        functools.partial(_mm_resid_kernel, g_idx=g_idx),
        grid=(n // tn, t // TM),
        in_specs=[pl.BlockSpec((TM, k), lambda j, i: (i, 0)),
                  pl.BlockSpec((None, k, tn), lambda j, i: (layer, 0, j)),
                  pl.BlockSpec((TM, tn), lambda j, i: (i, j)),
                  pl.BlockSpec((None, MOD_ROWS, tn), lambda j, i: (i * TM // SEG, 0, j))],
        out_specs=pl.BlockSpec((TM, tn), lambda j, i: (i, j)),
        out_shape=jax.ShapeDtypeStruct((t, n), F32),
        scratch_shapes=[pltpu.VMEM((k, tn), BF16)],
        compiler_params=_cparams(2), name="matmul_resid",
    )(h, w, x, modseg)


def _rope(x, cos, sin):
    lane = lax.broadcasted_iota(jnp.int32, x.shape, 1)
    first_half = (lane % DA_QK_DIM) < (DA_QK_DIM // 2)
    partner = jnp.where(first_half, pltpu.roll(x, LANES - DA_QK_DIM // 2, 1), pltpu.roll(x, DA_QK_DIM // 2, 1))
    return x * cos + partner * sin


def _store_values(vb_ref, v):
    vb_ref[:, 0:LANES] = v.astype(BF16)
    vb_ref[:, LANES:2 * LANES] = jnp.ones((v.shape[0], LANES), BF16)


def _softmax_values(q_bf, kb, vb, kcb, vcb, bias=None):
    dn = (((1,), (1,)), ((), ()))
    s = lax.dot_general(q_bf, kb, dn, preferred_element_type=F32)
    if bias is not None:
        s = s + bias
    m = jnp.max(s, axis=1, keepdims=True)
    if kcb is not None:
        sc = lax.dot_general(q_bf, kcb, dn, preferred_element_type=F32)
        m = jnp.maximum(m, jnp.max(sc, axis=1, keepdims=True))
    acc = jnp.dot(jnp.exp((s - m).astype(BF16)), vb, preferred_element_type=F32)
    if kcb is not None:
        acc += jnp.dot(jnp.exp((sc - m).astype(BF16)), vcb, preferred_element_type=F32)
    return acc[:, 0:LANES], acc[:, LANES:2 * LANES]


def _softmax_parts(q_bf, kb, kcb):
    dn = (((1,), (1,)), ((), ()))
    s = lax.dot_general(q_bf, kb, dn, preferred_element_type=F32)
    m = jnp.max(s, axis=1, keepdims=True)
    if kcb is not None:
        sc = lax.dot_general(q_bf, kcb, dn, preferred_element_type=F32)
        m = jnp.maximum(m, jnp.max(sc, axis=1, keepdims=True))
    p = jnp.exp(s - m)
    l = jnp.sum(p, axis=1, keepdims=True)
    pc = None
    if kcb is not None:
        pc = jnp.exp(sc - m)
        l = l + jnp.sum(pc, axis=1, keepdims=True)
    return p, pc, l


def _attn_kernel(*refs, diff, rope, n_ctx, scale, out_scale):
    it = iter(refs)
    q_ref, k_ref, v_ref = next(it), next(it), next(it)
    ck_ref = cv_ref = None
    if n_ctx:
        ck_ref, cv_ref = next(it), next(it)
    if rope:
        cosq_ref, sinq_ref, cosk_ref, sink_ref = next(it), next(it), next(it), next(it)
    if diff:
        lam_ref, g_ref = next(it), next(it)
    o_ref = next(it)
    kb_ref, vb_ref = next(it), next(it)
    if n_ctx:
        kcb_ref, vcb_ref = next(it), next(it)

    @pl.when(pl.program_id(2) == 0)
    def _():
        k = k_ref[...].astype(F32)
        if rope:
            k = _rope(k, cosk_ref[...], sink_ref[...])
        kb_ref[...] = k.astype(BF16)
        _store_values(vb_ref, v_ref[...])
        if n_ctx:
            kcb_ref[...] = ck_ref[...].astype(BF16)
            _store_values(vcb_ref, cv_ref[...])

    q = q_ref[...].astype(F32)
    if rope:
        q = _rope(q, cosq_ref[...], sinq_ref[...])
    q = q * scale
    kb = kb_ref[...]
    kcb = kcb_ref[...] if n_ctx else None
    if diff:
        lane = lax.broadcasted_iota(jnp.int32, q.shape, 1)
        p0, pc0, l0 = _softmax_parts(jnp.where(lane < DA_QK_DIM, q, 0.0).astype(BF16), kb, kcb)
        p1, pc1, l1 = _softmax_parts(jnp.where(lane >= DA_QK_DIM, q, 0.0).astype(BF16), kb, kcb)
        w0 = 1.0 / l0
        w1 = lam_ref[0:1, 0:1] / l1
        o = jnp.dot((p0 * w0 - p1 * w1).astype(BF16), vb_ref[:, 0:LANES], preferred_element_type=F32)
        if n_ctx:
            o += jnp.dot((pc0 * w0 - pc1 * w1).astype(BF16), vcb_ref[:, 0:LANES], preferred_element_type=F32)
        ms = jnp.mean(o * o, axis=-1, keepdims=True)
        o = (o * lax.rsqrt(ms + NORM_EPS) * g_ref[...]) * out_scale
    else:
        n0, l0 = _softmax_values(q.astype(BF16), kb, vb_ref[...], kcb, vcb_ref[...] if n_ctx else None)
        o = n0 / l0
    o_ref[...] = o.astype(o_ref.dtype)


def attention(qkv, *, row0, n_batch, n_seq, tq, diff, ctx=None, rope_tabs=None, lam=None, subln=None,
              out_scale=1.0, scale):
    nq = n_seq // tq
    qb0, kb0 = row0 // tq, row0 // n_seq
    diff_ = diff
    rope = rope_tabs is not None
    n_ctx = 0 if ctx is None else ctx[0].shape[2]
    in_specs = [
        pl.BlockSpec((tq, LANES), lambda b, h, i: (qb0 + b * nq + i, h)),
        pl.BlockSpec((n_seq, LANES), lambda b, h, i: (kb0 + b, HEADS + h)),
        pl.BlockSpec((n_seq, LANES), lambda b, h, i: (kb0 + b, 2 * HEADS + h)),
    ]
    args = [qkv, qkv, qkv]
    scratch = [pltpu.VMEM((n_seq, LANES), BF16), pltpu.VMEM((n_seq, 2 * LANES), BF16)]
    if n_ctx:
        ck, cv, cl = ctx
        cspec = pl.BlockSpec((None, None, n_ctx, LANES), lambda b, h, i: (b, cl, 0, h))
        in_specs += [cspec, cspec]
        args += [ck, cv]
        scratch += [pltpu.VMEM((n_ctx, LANES), BF16), pltpu.VMEM((n_ctx, 2 * LANES), BF16)]
    if rope:
        cos, sin = rope_tabs
        in_specs += [pl.BlockSpec((tq, LANES), lambda b, h, i: (i, 0))] * 2
        in_specs += [pl.BlockSpec((n_seq, LANES), lambda b, h, i: (0, 0))] * 2
        args += [cos, sin, cos, sin]
    if diff_:
        in_specs += [pl.BlockSpec((1, LANES), lambda b, h, i: (0, 0))] * 2
        args += [lam, subln]
    return pl.pallas_call(
        functools.partial(_attn_kernel, diff=diff_, rope=rope, n_ctx=n_ctx, scale=scale, out_scale=out_scale),
        grid=(n_batch, HEADS, nq),
        in_specs=in_specs,
        out_specs=pl.BlockSpec((tq, LANES), lambda b, h, i: (b * nq + i, h)),
        out_shape=jax.ShapeDtypeStruct((n_batch * n_seq, D_MODEL), BF16),
        scratch_shapes=scratch,
        compiler_params=_cparams(3), name="attention",
    )(*args)


def _na_key_row0(t):
    return jnp.clip(t * NA_TILE_ROWS - NA_WIN_ROWS // 2, 0, DEC_SEQ // GRID_W - NA_KEY_ROWS)


def _na_kernel(q_ref, k_ref, v_ref, ck_ref, cv_ref, bias_ref, o_ref, kb_ref, vb_ref, kcb_ref, vcb_ref, *, scale):
    t = pl.program_id(2)

    @pl.when(t == 0)
    def _():
        kb_ref[...] = k_ref[...].astype(BF16)
        _store_values(vb_ref, v_ref[...])
        kcb_ref[...] = ck_ref[...].astype(BF16)
        _store_values(vcb_ref, cv_ref[...])

    n_keys = NA_KEY_ROWS * GRID_W
    k0 = pl.multiple_of(_na_key_row0(t) * GRID_W, GRID_W)
    kw = kb_ref[pl.ds(k0, n_keys), :]
    vw = vb_ref[pl.ds(k0, n_keys), :]
    q = (q_ref[...].astype(F32) * scale).astype(BF16)
    n0, l0 = _softmax_values(q, kw, vw, kcb_ref[...], vcb_ref[...], bias=bias_ref[...])
    o_ref[...] = (n0 / l0).astype(o_ref.dtype)


def na_bias_table(rpb):
    rows = DEC_SEQ // GRID_W
    n_h = rpb.shape[0]
    col = jnp.arange(GRID_W)
    col0 = jnp.clip(col - NA_WIN_COLS // 2, 0, GRID_W - NA_WIN_COLS)
    col_ok = (col[None, :] >= col0[:, None]) & (col[None, :] < col0[:, None] + NA_WIN_COLS)
    dc_idx = jnp.clip(col[None, :] - col[:, None], -(NA_WIN_COLS - 1), NA_WIN_COLS - 1) + NA_WIN_COLS - 1
    pick = (dc_idx[None] == jnp.arange(2 * NA_WIN_COLS - 1)[:, None, None]).astype(F32)
    band = jnp.einsum('hrc,cqk->hrqk', rpb.astype(F32), pick, precision=lax.Precision.HIGHEST)
    band = jnp.where(col_ok, band, -jnp.inf)
    masked = jnp.full((n_h, GRID_W, GRID_W), -jnp.inf, F32)
    classes = []
    for tile in (0, 1, rows // NA_TILE_ROWS - 1):
        u0 = min(max(tile * NA_TILE_ROWS - NA_WIN_ROWS // 2, 0), rows - NA_KEY_ROWS)
        q_rows = []
        for jq in range(NA_TILE_ROWS):
            r = tile * NA_TILE_ROWS + jq
            r0 = min(max(r - NA_WIN_ROWS // 2, 0), rows - NA_WIN_ROWS)
            blocks = [band[:, u0 + u - r + NA_WIN_ROWS - 1] if r0 <= u0 + u < r0 + NA_WIN_ROWS else masked
                      for u in range(NA_KEY_ROWS)]
            q_rows.append(jnp.concatenate(blocks, axis=-1))
        classes.append(jnp.concatenate(q_rows, axis=1))
    return jnp.stack(classes, axis=1)


def na_attention(qkv, row0, cache_k, cache_v, cl, bias, scale):
    tq = NA_TILE_ROWS * GRID_W
    nt = DEC_SEQ // tq
    qb0, kb0 = row0 // tq, row0 // DEC_SEQ
    n_ctx = cache_k.shape[2]
    cspec = pl.BlockSpec((None, None, n_ctx, LANES), lambda b, h, t: (b, cl, 0, h))
    scratch = [pltpu.VMEM((DEC_SEQ, LANES), BF16), pltpu.VMEM((DEC_SEQ, 2 * LANES), BF16),
               pltpu.VMEM((n_ctx, LANES), BF16), pltpu.VMEM((n_ctx, 2 * LANES), BF16)]
    return pl.pallas_call(
        functools.partial(_na_kernel, scale=scale),
        grid=(DEC_BATCH, HEADS, nt),
        in_specs=[
            pl.BlockSpec((tq, LANES), lambda b, h, t: (qb0 + b * nt + t, h)),
            pl.BlockSpec((DEC_SEQ, LANES), lambda b, h, t: (kb0 + b, HEADS + h)),
            pl.BlockSpec((DEC_SEQ, LANES), lambda b, h, t: (kb0 + b, 2 * HEADS + h)),
            cspec, cspec,
            pl.BlockSpec((None, None, tq, NA_KEY_ROWS * GRID_W),
                         lambda b, h, t: (h, jnp.minimum(t, 1) + t // (nt - 1), 0, 0)),
        ],
        out_specs=pl.BlockSpec((tq, LANES), lambda b, h, t: (b * nt + t, h)),
        out_shape=jax.ShapeDtypeStruct((T_SAMPLE, D_MODEL), BF16),
        scratch_shapes=scratch,
        compiler_params=_cparams(3), name="na_attention",
    )(qkv, qkv, qkv, cache_k, cache_v, bias)


HG_LEVELS = HG_TILE.bit_length() - 1


def hgrn_masks():
    t = jnp.arange(HG_TILE)
    pair, qsel = [], []
    for d in range(2):
        for lvl in range(HG_LEVELS):
            upper = ((t >> lvl) & 1) == 1
            is_q = jnp.logical_not(upper) if d else upper
            same = (t[:, None] >> (lvl + 1)) == (t[None, :] >> (lvl + 1))
            pair.append(same & is_q[:, None] & jnp.logical_not(is_q)[None, :])
            qsel.append(jnp.broadcast_to(is_q[:, None], (HG_TILE, HG_TILE)))
    shape = (2, HG_LEVELS, HG_TILE, HG_TILE)
    return jnp.stack(pair).astype(F32).reshape(shape), jnp.stack(qsel).astype(F32).reshape(shape)


def _hgrn_tile(q, kk, f, v_f32, st_bf, d, pair_ref, qsel_ref):
    c = q.shape[0]
    rev = d == 1
    dn = (((1,), (1,)), ((), ()))
    px = f
    py = jnp.ones_like(f)
    pt = f
    att = jnp.zeros((c, c), F32)
    for lvl in range(HG_LEVELS):
        m = 1 << lvl
        qp = (q * px).astype(BF16)
        kp = (kk * py).astype(BF16)
        att = att + lax.dot_general(qp, kp, dn, preferred_element_type=F32) * pair_ref[d, lvl]
        below = pltpu.roll(pt, m, 0)
        above = pltpu.roll(pt, c - m, 0)
        is_q = qsel_ref[d, lvl] > 0.5
        sib = jnp.where(is_q, above, below) if rev else jnp.where(is_q, below, above)
        grown = jnp.where(is_q, px, py) * sib
        px = jnp.where(is_q, grown, px)
        py = jnp.where(is_q, py, grown)
        pt = pt * sib
    tot = pt[0:1, :]
    v_bf = v_f32.astype(BF16)
    diag = jnp.sum(q * kk, axis=1, keepdims=True)
    o = jnp.dot(att.astype(BF16), v_bf, preferred_element_type=F32) + diag * v_f32
    o += lax.dot_general((q * px).astype(BF16), st_bf, dn, preferred_element_type=F32)
    inc = jnp.dot(v_f32.T.astype(BF16), (kk * py).astype(BF16), preferred_element_type=F32)
    return o, inc, tot


def _hgrn_kernel(*refs, n, has_s0, emit_state):
    it = iter(refs)
    q_ref, i_ref, zf_ref, zb_ref, g_ref = next(it), next(it), next(it), next(it), next(it)
    s0_ref = next(it) if has_s0 else None
    lb_ref, gn_ref, pair_ref, qsel_ref = next(it), next(it), next(it), next(it)
    o_ref = next(it)
    sfin_ref = next(it) if emit_state else None
    of_ref, ob_ref, st_ref = next(it), next(it), next(it)
    nt = n // HG_TILE
    lb, one_m_lb = lb_ref[0:1, :], lb_ref[1:2, :]

    def gates(z):
        e = jnp.exp(-jnp.abs(z))
        r = 1.0 / (1.0 + e)
        pos = z > 0
        f = lb + one_m_lb * (jnp.where(pos, 1.0, e) * r)
        kk = one_m_lb * (jnp.where(pos, e, 1.0) * r)
        return f, kk

    for d in range(2):
        if has_s0:
            st_ref[d] = s0_ref[d].T
        else:
            st_ref[d] = jnp.zeros((HEAD_DIM, HEAD_DIM), F32)

    def scan_step(step, carry):
        for d, (z_ref, out_ref) in enumerate(((zf_ref, of_ref), (zb_ref, ob_ref))):
            ti = (nt - 1 - step) if d else step
            rows = pl.ds(pl.multiple_of(ti * HG_TILE, HG_TILE), HG_TILE)
            f, kk = gates(z_ref[rows, :])
            qr = q_ref[rows, :].astype(F32)
            q = (qr / (1.0 + jnp.exp(-qr))) * (HEAD_DIM ** -0.5)
            st = st_ref[d]
            o, inc, tot = _hgrn_tile(q, kk, f, i_ref[rows, :].astype(F32), st.astype(BF16), d,
                                     pair_ref, qsel_ref)
            st_ref[d] = st * tot + inc
            out_ref[rows, :] = o
        return carry

    lax.fori_loop(0, nt, scan_step, 0)

    def finish(ti, carry):
        rows = pl.ds(pl.multiple_of(ti * HG_TILE, HG_TILE), HG_TILE)
        o = of_ref[rows, :] + ob_ref[rows, :]
        ms = jnp.mean(o * o, axis=-1, keepdims=True)
        o = o * lax.rsqrt(ms + NORM_EPS) * gn_ref[...]
        g = g_ref[rows, :].astype(F32)
        o_ref[rows, :] = (o * (g / (1.0 + jnp.exp(-g)))).astype(o_ref.dtype)
        return carry

    lax.fori_loop(0, nt, finish, 0)
    if emit_state:
        for d in range(2):
            sfin_ref[d] = st_ref[d].T


def hgrn_scan(qi, z, g, lbp, layer, gnorm, masks, *, row0, n_batch, n_seq, s0=None, s0_layer=0, emit_state=False):
    rb0 = row0 // n_seq
    blk = lambda c0: pl.BlockSpec((n_seq, LANES), lambda b, h: (rb0 + b, c0 + h))
    in_specs = [blk(0), blk(HEADS), blk(0), blk(HEADS), blk(0)]
    args = [qi, qi, z, z, g]
    if s0 is not None:
        in_specs.append(pl.BlockSpec((None, None, 2, None, HEAD_DIM, HEAD_DIM),
                                     lambda b, h: (b, s0_layer, 0, h, 0, 0)))
        args.append(s0)
    mask_spec = pl.BlockSpec((2, HG_LEVELS, HG_TILE, HG_TILE), lambda b, h: (0, 0, 0, 0))
    in_specs += [pl.BlockSpec((None, MOD_ROWS, LANES), lambda b, h: (layer, 0, h)),
                 pl.BlockSpec((1, LANES), lambda b, h: (0, 0)), mask_spec, mask_spec]
    args += [lbp, gnorm, masks[0], masks[1]]
    out_specs = [pl.BlockSpec((n_seq, LANES), lambda b, h: (b, h))]
    out_shape = [jax.ShapeDtypeStruct((n_batch * n_seq, D_MODEL), BF16)]
    if emit_state:
        out_specs.append(pl.BlockSpec((None, 2, None, HEAD_DIM, HEAD_DIM), lambda b, h: (b, 0, h, 0, 0)))
        out_shape.append(jax.ShapeDtypeStruct((n_batch, 2, HEADS, HEAD_DIM, HEAD_DIM), F32))
    res = pl.pallas_call(
        functools.partial(_hgrn_kernel, n=n_seq, has_s0=s0 is not None, emit_state=emit_state),
        grid=(n_batch, HEADS),
        in_specs=in_specs, out_specs=out_specs, out_shape=out_shape,
        scratch_shapes=[pltpu.VMEM((n_seq, LANES), F32), pltpu.VMEM((n_seq, LANES), F32),
                        pltpu.VMEM((2, HEAD_DIM, HEAD_DIM), F32)],
        compiler_params=_cparams(2), name="hgrn_scan",
    )(*args)
    return res if emit_state else res[0]


def _expert_kernel(be_ref, first_ref, nused_ref, x_ref, wg_ref, wu_ref, wd_ref, o_ref, wg_bf, wu_bf, wd_bf):
    g = pl.program_id(0)

    @pl.when(g < nused_ref[0])
    def _():
        @pl.when(first_ref[g] == 1)
        def _():
            wg_bf[...] = wg_ref[...].astype(BF16)
            wu_bf[...] = wu_ref[...].astype(BF16)
            wd_bf[...] = wd_ref[...].astype(BF16)

        x = x_ref[...].astype(BF16)
        a = jnp.dot(x, wg_bf[...], preferred_element_type=F32)
        u = jnp.dot(x, wu_bf[...], preferred_element_type=F32)
        hmid = (a / (1.0 + jnp.exp(-a))) * u
        o_ref[...] = jnp.dot(hmid.astype(BF16), wd_bf[...], preferred_element_type=F32).astype(o_ref.dtype)

    @pl.when(g >= nused_ref[0])
    def _():
        o_ref[...] = jnp.zeros_like(o_ref)


def experts(xb, block_e, first, n_used, w_gate, w_up, w_down, layer):
    cap, d = xb.shape
    nb = cap // MOE_ROWS
    de = w_gate.shape[-1]
    grid_spec = pltpu.PrefetchScalarGridSpec(
        num_scalar_prefetch=3,
        grid=(nb,),
        in_specs=[
            pl.BlockSpec((MOE_ROWS, d), lambda g, be, fi, nu: (g, 0)),
            pl.BlockSpec((None, None, d, de), lambda g, be, fi, nu: (layer, be[g], 0, 0)),
            pl.BlockSpec((None, None, d, de), lambda g, be, fi, nu: (layer, be[g], 0, 0)),
            pl.BlockSpec((None, None, de, d), lambda g, be, fi, nu: (layer, be[g], 0, 0)),
        ],
        out_specs=pl.BlockSpec((MOE_ROWS, d), lambda g, be, fi, nu: (g, 0)),
        scratch_shapes=[pltpu.VMEM((d, de), BF16), pltpu.VMEM((d, de), BF16), pltpu.VMEM((de, d), BF16)],
    )
    return pl.pallas_call(
        _expert_kernel, grid_spec=grid_spec,
        out_shape=jax.ShapeDtypeStruct((cap, d), F32),
        compiler_params=_cparams(1), name="experts",
    )(block_e, first, n_used, xb, w_gate, w_up, w_down)


def _dispatch_kernel(pos_ref, segend_ref, cnt_ref, nused_ref, h_hbm, xb_hbm, zero_ref, sem, zsem):
    i = pl.program_id(0)
    n_a = pos_ref.shape[0]
    rows = n_a // 2
    n_blocks = xb_hbm.shape[0] // MOE_ROWS

    def zero_block(start):
        return pltpu.make_async_copy(zero_ref, xb_hbm.at[pl.ds(pl.multiple_of(start, MOE_ROWS), MOE_ROWS), :], zsem)

    @pl.when(i == 0)
    def _():
        zero_ref[...] = jnp.zeros_like(zero_ref)

        def start(e, c):
            @pl.when(cnt_ref[e] > 0)
            def _():
                zero_block(segend_ref[e] - MOE_ROWS).start()
            return c

        def wait(e, c):
            @pl.when(cnt_ref[e] > 0)
            def _():
                zero_block(segend_ref[e] - MOE_ROWS).wait()
            return c

        def start_tail(g, c):
            zero_block(g * MOE_ROWS).start()
            return c

        def wait_tail(g, c):
            zero_block(g * MOE_ROWS).wait()
            return c

        lax.fori_loop(0, MOE_EXPERTS, start, 0)
        lax.fori_loop(nused_ref[0], n_blocks, start_tail, 0)
        lax.fori_loop(0, MOE_EXPERTS, wait, 0)
        lax.fori_loop(nused_ref[0], n_blocks, wait_tail, 0)

    def issue(a, c):
        t = i * rows + a // 2
        pltpu.make_async_copy(h_hbm.at[pl.ds(t, 1), :], xb_hbm.at[pl.ds(pos_ref[a], 1), :], sem).start()
        return c

    lax.fori_loop(0, n_a, issue, 0, unroll=8)
    pltpu.make_async_copy(h_hbm.at[pl.ds(0, n_a), :], xb_hbm.at[pl.ds(0, n_a), :], sem).wait()


def dispatch(h, pos_flat, seg_end, counts, n_used, cap):
    t, d = h.shape
    rows = 512
    grid_spec = pltpu.PrefetchScalarGridSpec(
        num_scalar_prefetch=0,
        grid=(t // rows,),
        in_specs=[pl.BlockSpec((2 * rows,), lambda i: (i,), memory_space=pltpu.SMEM),
                  pl.BlockSpec(memory_space=pltpu.SMEM),
                  pl.BlockSpec(memory_space=pltpu.SMEM),
                  pl.BlockSpec(memory_space=pltpu.SMEM),
                  pl.BlockSpec(memory_space=pl.ANY)],
        out_specs=pl.BlockSpec(memory_space=pl.ANY),
        scratch_shapes=[pltpu.VMEM((MOE_ROWS, d), F32), pltpu.SemaphoreType.DMA(()), pltpu.SemaphoreType.DMA(())],
    )
    return pl.pallas_call(
        _dispatch_kernel, grid_spec=grid_spec,
        out_shape=jax.ShapeDtypeStruct((cap, d), F32),
        compiler_params=_cparams(1), name="dispatch",
    )(pos_flat, seg_end, counts, n_used, h)


def _combine_kernel(pos_ref, x_ref, r_ref, mod_ref, gain_ref, modn_ref, yb_hbm, *rest, g_idx, final):
    if final:
        y_ref, buf0, buf1, sem = rest
    else:
        o_ref, h_ref, buf0, buf1, sem = rest
    tm = x_ref.shape[0]

    def issue(r, c):
        pltpu.make_async_copy(yb_hbm.at[pl.ds(pos_ref[2 * r], 1), :], buf0.at[pl.ds(r, 1), :], sem).start()
        pltpu.make_async_copy(yb_hbm.at[pl.ds(pos_ref[2 * r + 1], 1), :], buf1.at[pl.ds(r, 1), :], sem).start()
        return c

    lax.fori_loop(0, tm, issue, 0, unroll=8)
    pltpu.make_async_copy(yb_hbm.at[pl.ds(0, tm), :], buf0, sem).wait()
    pltpu.make_async_copy(yb_hbm.at[pl.ds(0, tm), :], buf1, sem).wait()
    r = r_ref[...]
    y = r[:, 0:1] * buf0[...] + r[:, 1:2] * buf1[...]
    x = x_ref[...] + mod_ref[g_idx:g_idx + 1, :] * y
    if final:
        ms = jnp.mean(x * x, axis=-1, keepdims=True)
        y_ref[...] = x * lax.rsqrt(ms + NORM_EPS) * gain_ref[...]
    else:
        o_ref[...] = x
        h_ref[...] = _norm_mod(x, gain_ref[...], modn_ref[...], 0, 1).astype(BF16)


def combine(x, yb, pos_flat, route, modseg, g_idx, gains, gain_idx, modseg_next, final):
    t, d = x.shape
    tm = 256
    row_spec = pl.BlockSpec((tm, d), lambda i: (i, 0))
    mod_spec = pl.BlockSpec((None, MOD_ROWS, d), lambda i: (i * tm // SEG, 0, 0))
    if final:
        out_specs, out_shape = row_spec, jax.ShapeDtypeStruct((t, d), F32)
    else:
        out_specs = [row_spec, row_spec]
        out_shape = [jax.ShapeDtypeStruct((t, d), F32), jax.ShapeDtypeStruct((t, d), BF16)]
    grid_spec = pltpu.PrefetchScalarGridSpec(
        num_scalar_prefetch=0,
        grid=(t // tm,),
        in_specs=[pl.BlockSpec((2 * tm,), lambda i: (i,), memory_space=pltpu.SMEM),
                  row_spec,
                  pl.BlockSpec((tm, LANES), lambda i: (i, 0)),
                  mod_spec,
                  pl.BlockSpec((None, 1, d), lambda i: (gain_idx, 0, 0)),
                  mod_spec,
                  pl.BlockSpec(memory_space=pl.ANY)],
        out_specs=out_specs,
        scratch_shapes=[pltpu.VMEM((tm, d), F32), pltpu.VMEM((tm, d), F32), pltpu.SemaphoreType.DMA(())],
    )
    return pl.pallas_call(
        functools.partial(_combine_kernel, g_idx=g_idx, final=final), grid_spec=grid_spec, out_shape=out_shape,
        compiler_params=_cparams(1), name="combine",
    )(pos_flat, x, route, modseg, gains.reshape(gains.shape[0], 1, d), modseg_next, yb)


def moe_plan(route, cnt):
    t = route.shape[0]
    counts = cnt[0, :MOE_EXPERTS].astype(jnp.int32)
    padded = (counts + MOE_ROWS - 1) // MOE_ROWS * MOE_ROWS
    seg_end = jnp.cumsum(padded).astype(jnp.int32)
    seg_start = seg_end - padded
    e = route[:, 2:4].astype(jnp.int32)
    rank = route[:, 4:6].astype(jnp.int32)
    onehot = e[:, :, None] == jnp.arange(MOE_EXPERTS, dtype=jnp.int32)
    pos = rank + jnp.sum(jnp.where(onehot, seg_start, 0), axis=-1)
    n_blocks = -(-(2 * t + MOE_EXPERTS * (MOE_ROWS - 1)) // MOE_ROWS)
    blk0 = jnp.arange(n_blocks, dtype=jnp.int32) * MOE_ROWS
    block_e = jnp.minimum(jnp.sum(seg_end[None, :] <= blk0[:, None], axis=1), MOE_EXPERTS - 1).astype(jnp.int32)
    first = jnp.concatenate([jnp.ones((1,), jnp.int32), (block_e[1:] != block_e[:-1]).astype(jnp.int32)])
    n_used = (seg_end[-1] // MOE_ROWS).astype(jnp.int32).reshape(1)
    return pos.reshape(-1).astype(jnp.int32), seg_end, counts, block_e, first, n_used, n_blocks * MOE_ROWS


def _rope_tables(n, dim):
    quarter = dim // 4
    inv = jnp.power(ROPE_BASE, -jnp.arange(quarter, dtype=F32) / quarter)
    t = jnp.arange(n)
    row = (t // GRID_W).astype(F32)
    col = (t % GRID_W).astype(F32)
    ang = jnp.concatenate([row[:, None] * inv, col[:, None] * inv], axis=-1)
    cos, sin = jnp.cos(ang), jnp.sin(ang)
    reps = LANES // dim
    cos_l = jnp.tile(jnp.concatenate([cos, cos], axis=-1), (1, reps))
    sin_l = jnp.tile(jnp.concatenate([-sin, sin], axis=-1), (1, reps))
    return cos_l, sin_l


def kernel(x_prompt, x_sample, state_hgrn, cache_diff_k, cache_diff_v, cache_na_k, cache_na_v, c, c_ctx, norm_mix, norm_ffn, w_ada, b_ada, router_group_w, router_group_b, router_expert_w, router_expert_b, expert_w_gate, expert_w_up, expert_w_down, hg_w_in, hg_lb_logits, hg_gnorm, hg_w_out, da_w_qkv, da_lambda, da_subln, da_w_out, na_w_qkv, na_rpb, na_w_out, final_norm):
    d = D_MODEL
    x = jnp.concatenate([x_prompt.reshape(T_PROMPT, d), x_sample.reshape(T_SAMPLE, d)], axis=0)

    cond8 = jnp.zeros((MOD_ROWS, d), F32).at[0].set(c_ctx).at[1:1 + DEC_BATCH].set(c)
    mod_all = ada_all(cond8, w_ada, b_ada)
    seg_cond = jnp.array([0] * (T_PROMPT // SEG) + list(range(1, 1 + DEC_BATCH)), jnp.int32)
    mod_all = mod_all.reshape(DEPTH, MOD_ROWS, N_MOD, d)[:, seg_cond]
    mod_all = jnp.pad(mod_all, ((0, 0), (0, 0), (0, MOD_ROWS - N_MOD), (0, 0)))

    p = jax.nn.softmax(hg_lb_logits.astype(F32), axis=0)
    cum = jnp.cumsum(p, axis=0)
    lb = cum - cum[0:1]
    lbp = jnp.stack([lb, 1.0 - lb], axis=1)
    lbp = jnp.pad(lbp, ((0, 0), (0, MOD_ROWS - 2), (0, 0)))

    wr = jnp.concatenate([router_group_w, router_expert_w], axis=-1)
    wr = jnp.pad(wr, ((0, 0), (0, 0), (0, LANES - wr.shape[-1])))
    wr_hi = wr.astype(BF16)
    wr_lo = (wr - wr_hi.astype(F32)).astype(BF16)
    br = jnp.concatenate([router_group_b, router_expert_b], axis=-1).astype(F32)
    br = jnp.pad(br, ((0, 0), (0, LANES - br.shape[-1])))[:, None, :]

    hg_masks = hgrn_masks()
    cos_l, sin_l = _rope_tables(DEC_SEQ, DA_QK_DIM)
    cdk = cache_diff_k.reshape(DEC_BATCH, -1, PAST_LEN, d)
    cdv = cache_diff_v.reshape(DEC_BATCH, -1, PAST_LEN, d)
    cnk = cache_na_k.reshape(DEC_BATCH, -1, PAST_LEN, d)
    cnv = cache_na_v.reshape(DEC_BATCH, -1, PAST_LEN, d)

    new_hg, new_dk, new_dv, new_nk, new_nv = [], [], [], [], []
    h = norm_mod(x, norm_mix, 0, mod_all[0], 0, 1)
    for i in range(DEPTH):
        kind, j = i % N_MIXERS, i // N_MIXERS
        modseg = mod_all[i]
        if kind == 0:
            qi = matmul(h, hg_w_in, j, 0, 2 * d, BF16)
            z = matmul(h, hg_w_in, j, 2 * d, 2 * d, F32)
            g = matmul(h, hg_w_in, j, 4 * d, d, BF16)
            gn = hg_gnorm[j].reshape(1, HEAD_DIM)
            op, s_fin = hgrn_scan(qi, z, g, lbp, i, gn, hg_masks, row0=0, n_batch=BATCH, n_seq=SEQ, emit_state=True)
            os_ = hgrn_scan(qi, z, g, lbp, i, gn, hg_masks, row0=T_PROMPT, n_batch=DEC_BATCH, n_seq=DEC_SEQ,
                            s0=state_hgrn, s0_layer=j)
            new_hg.append(s_fin)
            o = jnp.concatenate([op, os_], axis=0)
            w_out = hg_w_out
        elif kind == 1:
            lam_init = 0.8 - 0.6 * math.exp(-0.3 * i)
            lp = da_lambda[j].astype(F32)
            lam = jnp.exp(jnp.sum(lp[0] * lp[1])) - jnp.exp(jnp.sum(lp[2] * lp[3])) + lam_init
            lam_row = jnp.full((1, LANES), lam, F32)
            subln = da_subln[j].reshape(1, HEAD_DIM).astype(F32)
            qkv = matmul(h, da_w_qkv, j, 0, 3 * d, F32)
            scale = DA_QK_DIM ** -0.5
            op = attention(qkv, row0=0, n_batch=BATCH, n_seq=SEQ, tq=SEQ, diff=True, lam=lam_row, subln=subln,
                           out_scale=1.0 - lam_init, scale=scale)
            os_ = attention(qkv, row0=T_PROMPT, n_batch=DEC_BATCH, n_seq=DEC_SEQ, tq=256, diff=True,
                            ctx=(cdk, cdv, j), rope_tabs=(cos_l, sin_l), lam=lam_row, subln=subln,
                            out_scale=1.0 - lam_init, scale=scale)
            new_dk.append(qkv[:T_PROMPT, d:2 * d].reshape(BATCH, SEQ, HEADS, 2, DA_QK_DIM))
            new_dv.append(qkv[:T_PROMPT, 2 * d:].reshape(BATCH, SEQ, HEADS, HEAD_DIM))
            o = jnp.concatenate([op, os_], axis=0)
            w_out = da_w_out
        else:
            qkv = matmul(h, na_w_qkv, j, 0, 3 * d, F32)
            scale = HEAD_DIM ** -0.5
            op = attention(qkv, row0=0, n_batch=BATCH, n_seq=SEQ, tq=SEQ, diff=False, scale=scale)
            os_ = na_attention(qkv, T_PROMPT, cnk, cnv, j, na_bias_table(na_rpb[j]), scale)
            new_nk.append(qkv[:T_PROMPT, d:2 * d].reshape(BATCH, SEQ, HEADS, HEAD_DIM))
            new_nv.append(qkv[:T_PROMPT, 2 * d:].reshape(BATCH, SEQ, HEADS, HEAD_DIM))
            o = jnp.concatenate([op, os_], axis=0)
            w_out = na_w_out
        x = matmul_resid(o, w_out, j, x, modseg, 2)

        hf, route, cnt = norm_mod(x, norm_ffn, i, modseg, 3, 4, router=(wr_hi[i], wr_lo[i], br[i]))
        pos, seg_end, counts, block_e, first, n_used, cap = moe_plan(route, cnt)
        xb = dispatch(hf, pos, seg_end, counts, n_used, cap)
        yb = experts(xb, block_e, first, n_used, expert_w_gate, expert_w_up, expert_w_down, i)
        if i + 1 < DEPTH:
            x, h = combine(x, yb, pos, route, modseg, 5, norm_mix, i + 1, mod_all[i + 1], final=False)
        else:
            y = combine(x, yb, pos, route, modseg, 5, final_norm.reshape(1, d), 0, modseg, final=True)

    return (y[:T_PROMPT].reshape(BATCH, SEQ, d), y[T_PROMPT:].reshape(DEC_BATCH, DEC_SEQ, d),
            jnp.stack(new_hg, axis=1), jnp.stack(new_dk, axis=1), jnp.stack(new_dv, axis=1),
            jnp.stack(new_nk, axis=1), jnp.stack(new_nv, axis=1))
```

```python
import functools
import math

import jax
import jax.numpy as jnp
from jax import lax
from jax.experimental import pallas as pl
from jax.experimental.pallas import tpu as pltpu

D_MODEL = 2048
BATCH = 32
SEQ = 256
DEPTH = 4
DEC_BATCH = 4
DEC_SEQ = 4096
PAST_LEN = 512
GRID_W = 64
N_MIXERS = 3
HEADS = 16
HEAD_DIM = D_MODEL // HEADS
DA_QK_DIM = 64
NA_WIN_ROWS = 8
NA_WIN_COLS = 16
MOE_GROUPS = 4
MOE_EXPERTS_PER_GROUP = 8
MOE_EXPERTS = MOE_GROUPS * MOE_EXPERTS_PER_GROUP
MOE_D_EXPERT = 512
ROPE_BASE = 10000.0
NORM_EPS = 1e-6
N_MOD = 6

T_PROMPT = BATCH * SEQ
T_SAMPLE = DEC_BATCH * DEC_SEQ
T_ALL = T_PROMPT + T_SAMPLE
SEG = DEC_SEQ
N_SEG = T_ALL // SEG
MOD_ROWS = 8

LANES = 128
VMEM_LIMIT = 56 * 1024 * 1024

TM = 512
TN = 1024
MOE_ROWS = 256
HG_TILE = 128
NA_TILE_ROWS = 8
NA_KEY_ROWS = 16

F32 = jnp.float32
BF16 = jnp.bfloat16


def _cparams(n_axes):
    return pltpu.CompilerParams(dimension_semantics=("arbitrary",) * n_axes, vmem_limit_bytes=VMEM_LIMIT)


def _split_bf16(x):
    hi = x.astype(BF16)
    lo = (x - hi.astype(F32)).astype(BF16)
    return hi, lo


def _ada_kernel(cond_ref, w_ref, b_ref, o_ref):
    a = cond_ref[...]
    a = a / (1.0 + jnp.exp(-a))
    a_hi, a_lo = _split_bf16(a)
    w_hi, w_lo = _split_bf16(w_ref[...])
    acc = jnp.dot(a_hi, w_hi, preferred_element_type=F32)
    acc += jnp.dot(a_lo, w_hi, preferred_element_type=F32)
    acc += jnp.dot(a_hi, w_lo, preferred_element_type=F32)
    o_ref[...] = acc + b_ref[...]


def ada_all(cond8, w_ada, b_ada):
    depth, d, n = w_ada.shape
    tn = 1024
    return pl.pallas_call(
        _ada_kernel,
        grid=(depth, n // tn),
        in_specs=[
            pl.BlockSpec((MOD_ROWS, d), lambda l, j: (0, 0)),
            pl.BlockSpec((None, d, tn), lambda l, j: (l, 0, j)),
            pl.BlockSpec((None, 1, tn), lambda l, j: (l, 0, j)),
        ],
        out_specs=pl.BlockSpec((None, MOD_ROWS, tn), lambda l, j: (l, 0, j)),
        out_shape=jax.ShapeDtypeStruct((depth, MOD_ROWS, n), F32),
        compiler_params=_cparams(2),
        name="ada",
    )(cond8, w_ada, b_ada.reshape(depth, 1, n))


def _norm_mod(x, gain, mod, sh_idx, sc_idx):
    ms = jnp.mean(x * x, axis=-1, keepdims=True)
    y = x * lax.rsqrt(ms + NORM_EPS) * gain
    return y * (1.0 + mod[sc_idx:sc_idx + 1, :]) + mod[sh_idx:sh_idx + 1, :]


def _norm_kernel(x_ref, gain_ref, mod_ref, h_ref, *, sh_idx, sc_idx):
    h_ref[...] = _norm_mod(x_ref[...], gain_ref[...], mod_ref[...], sh_idx, sc_idx).astype(BF16)


def _route(logits):
    lane = lax.broadcasted_iota(jnp.int32, logits.shape, 1)
    neg = jnp.float32(-jnp.inf)
    big = jnp.int32(LANES)
    lg = jnp.where(lane < MOE_GROUPS, logits, neg)
    mg = jnp.max(lg, axis=1, keepdims=True)
    grp = jnp.min(jnp.where(lg == mg, lane, big), axis=1, keepdims=True)
    p_grp = 1.0 / jnp.sum(jnp.exp(lg - mg), axis=1, keepdims=True)
    lo = MOE_GROUPS + grp * MOE_EXPERTS_PER_GROUP
    le = jnp.where((lane >= lo) & (lane < lo + MOE_EXPERTS_PER_GROUP), logits, neg)
    m1 = jnp.max(le, axis=1, keepdims=True)
    i1 = jnp.min(jnp.where(le == m1, lane, big), axis=1, keepdims=True)
    le2 = jnp.where(lane == i1, neg, le)
    m2 = jnp.max(le2, axis=1, keepdims=True)
    i2 = jnp.min(jnp.where(le2 == m2, lane, big), axis=1, keepdims=True)
    p2 = jnp.exp(m2 - m1)
    g1 = p_grp / (1.0 + p2)
    g2 = p_grp * p2 / (1.0 + p2)
    return g1, g2, i1 - MOE_GROUPS, i2 - MOE_GROUPS


def _norm_route_kernel(x_ref, gain_ref, mod_ref, wr_hi_ref, wr_lo_ref, br_ref, h_ref, r_ref, cnt_ref, run_ref,
                       *, sh_idx, sc_idx):
    @pl.when(pl.program_id(0) == 0)
    def _():
        run_ref[...] = jnp.zeros_like(run_ref)

    h = _norm_mod(x_ref[...], gain_ref[...], mod_ref[...], sh_idx, sc_idx)
    h_ref[...] = h
    h_hi, h_lo = _split_bf16(h)
    w_hi = wr_hi_ref[...]
    logits = jnp.dot(h_hi, w_hi, preferred_element_type=F32)
    logits += jnp.dot(h_lo, w_hi, preferred_element_type=F32)
    logits += jnp.dot(h_hi, wr_lo_ref[...], preferred_element_type=F32)
    g1, g2, e1, e2 = _route(logits + br_ref[...])

    tm = h.shape[0]
    lane = lax.broadcasted_iota(jnp.int32, (tm, LANES), 1)
    o1 = (lane == e1).astype(F32)
    o2 = (lane == e2).astype(F32)
    before = (lax.broadcasted_iota(jnp.int32, (tm, tm), 0) > lax.broadcasted_iota(jnp.int32, (tm, tm), 1)).astype(BF16)
    p1 = jnp.dot(before, o1.astype(BF16), preferred_element_type=F32)
    p2 = jnp.dot(before, o2.astype(BF16), preferred_element_type=F32)
    tot1 = jnp.sum(o1, axis=0, keepdims=True)
    tot2 = jnp.sum(o2, axis=0, keepdims=True)
    run = run_ref[0:1, :]
    rank1 = jnp.sum(o1 * (run + p1), axis=1, keepdims=True)
    rank2 = jnp.sum(o2 * (run + tot1 + p2), axis=1, keepdims=True)
    run = run + tot1 + tot2
    run_ref[0:1, :] = run
    cnt_ref[...] = jnp.broadcast_to(run, cnt_ref.shape)

    out = jnp.where(lane == 0, g1, 0.0)
    out = jnp.where(lane == 1, g2, out)
    out = jnp.where(lane == 2, e1.astype(F32), out)
    out = jnp.where(lane == 3, e2.astype(F32), out)
    out = jnp.where(lane == 4, rank1, out)
    out = jnp.where(lane == 5, rank2, out)
    r_ref[...] = out


def norm_mod(x, gains, layer, modseg, sh_idx, sc_idx, router=None):
    t, d = x.shape
    in_specs = [
        pl.BlockSpec((TM, d), lambda i: (i, 0)),
        pl.BlockSpec((None, 1, d), lambda i: (layer, 0, 0)),
        pl.BlockSpec((None, MOD_ROWS, d), lambda i: (i * TM // SEG, 0, 0)),
    ]
    h_spec = pl.BlockSpec((TM, d), lambda i: (i, 0))
    h_shape = jax.ShapeDtypeStruct((t, d), BF16)
    gains3 = gains.reshape(gains.shape[0], 1, d)
    if router is None:
        return pl.pallas_call(
            functools.partial(_norm_kernel, sh_idx=sh_idx, sc_idx=sc_idx),
            grid=(t // TM,), in_specs=in_specs, out_specs=h_spec, out_shape=h_shape,
            compiler_params=_cparams(1), name="norm_mod",
        )(x, gains3, modseg)
    wr_hi, wr_lo, br = router
    const = lambda i: (0, 0)
    return pl.pallas_call(
        functools.partial(_norm_route_kernel, sh_idx=sh_idx, sc_idx=sc_idx),
        grid=(t // TM,),
        in_specs=in_specs + [pl.BlockSpec((d, LANES), const), pl.BlockSpec((d, LANES), const),
                             pl.BlockSpec((1, LANES), const)],
        out_specs=[h_spec, pl.BlockSpec((TM, LANES), lambda i: (i, 0)), pl.BlockSpec((MOD_ROWS, LANES), const)],
        out_shape=[jax.ShapeDtypeStruct((t, d), F32), jax.ShapeDtypeStruct((t, LANES), F32),
                   jax.ShapeDtypeStruct((MOD_ROWS, LANES), F32)],
        scratch_shapes=[pltpu.VMEM((MOD_ROWS, LANES), F32)],
        compiler_params=_cparams(1), name="norm_route",
    )(x, gains3, modseg, wr_hi, wr_lo, br)


def _mm_kernel(h_ref, w_ref, o_ref, wbf_ref):
    @pl.when(pl.program_id(1) == 0)
    def _():
        wbf_ref[...] = w_ref[...].astype(BF16)

    o_ref[...] = jnp.dot(h_ref[...], wbf_ref[...], preferred_element_type=F32).astype(o_ref.dtype)


def _mm_resid_kernel(h_ref, w_ref, x_ref, mod_ref, o_ref, wbf_ref, *, g_idx):
    @pl.when(pl.program_id(1) == 0)
    def _():
        wbf_ref[...] = w_ref[...].astype(BF16)

    acc = jnp.dot(h_ref[...], wbf_ref[...], preferred_element_type=F32)
    o_ref[...] = x_ref[...] + mod_ref[g_idx:g_idx + 1, :] * acc


def matmul(h, w, layer, col0, n_cols, out_dtype, row0=0, n_rows=None):
    k = h.shape[1]
    n_rows = h.shape[0] - row0 if n_rows is None else n_rows
    tn = min(TN, n_cols)
    r0, c0 = row0 // TM, col0 // tn
    return pl.pallas_call(
        _mm_kernel,
        grid=(n_cols // tn, n_rows // TM),
        in_specs=[pl.BlockSpec((TM, k), lambda j, i: (r0 + i, 0)),
                  pl.BlockSpec((None, k, tn), lambda j, i: (layer, 0, c0 + j))],
        out_specs=pl.BlockSpec((TM, tn), lambda j, i: (i, j)),
        out_shape=jax.ShapeDtypeStruct((n_rows, n_cols), out_dtype),
        scratch_shapes=[pltpu.VMEM((k, tn), BF16)],
        compiler_params=_cparams(2), name="matmul",
    )(h, w)


def matmul_resid(h, w, layer, x, modseg, g_idx):
    t, k = h.shape
    n = w.shape[-1]
    tn = min(TN, n)
    return pl.pallas_call(
        functools.partial(_mm_resid_kernel, g_idx=g_idx),
        grid=(n // tn, t // TM),
        in_specs=[pl.BlockSpec((TM, k), lambda j, i: (i, 0)),
                  pl.BlockSpec((None, k, tn), lambda j, i: (layer, 0, j)),
                  pl.BlockSpec((TM, tn), lambda j, i: (i, j)),
                  pl.BlockSpec((None, MOD_ROWS, tn), lambda j, i: (i * TM // SEG, 0, j))],
        out_specs=pl.BlockSpec((TM, tn), lambda j, i: (i, j)),
        out_shape=jax.ShapeDtypeStruct((t, n), F32),
        scratch_shapes=[pltpu.VMEM((k, tn), BF16)],
        compiler_params=_cparams(2), name="matmul_resid",
    )(h, w, x, modseg)


def _rope(x, cos, sin):
    lane = lax.broadcasted_iota(jnp.int32, x.shape, 1)
    first_half = (lane % DA_QK_DIM) < (DA_QK_DIM // 2)
    partner = jnp.where(first_half, pltpu.roll(x, LANES - DA_QK_DIM // 2, 1), pltpu.roll(x, DA_QK_DIM // 2, 1))
    return x * cos + partner * sin


def _store_values(vb_ref, v):
    vb_ref[:, 0:LANES] = v.astype(BF16)
    vb_ref[:, LANES:2 * LANES] = jnp.ones((v.shape[0], LANES), BF16)


def _softmax_values(q_bf, kb, vb, kcb, vcb, bias=None):
    dn = (((1,), (1,)), ((), ()))
    s = lax.dot_general(q_bf, kb, dn, preferred_element_type=F32)
    if bias is not None:
        s = s + bias
    m = jnp.max(s, axis=1, keepdims=True)
    if kcb is not None:
        sc = lax.dot_general(q_bf, kcb, dn, preferred_element_type=F32)
        m = jnp.maximum(m, jnp.max(sc, axis=1, keepdims=True))
    acc = jnp.dot(jnp.exp((s - m).astype(BF16)), vb, preferred_element_type=F32)
    if kcb is not None:
        acc += jnp.dot(jnp.exp((sc - m).astype(BF16)), vcb, preferred_element_type=F32)
    return acc[:, 0:LANES], acc[:, LANES:2 * LANES]


def _softmax_parts(q_bf, kb, kcb):
    dn = (((1,), (1,)), ((), ()))
    s = lax.dot_general(q_bf, kb, dn, preferred_element_type=F32)
    m = jnp.max(s, axis=1, keepdims=True)
    if kcb is not None:
        sc = lax.dot_general(q_bf, kcb, dn, preferred_element_type=F32)
        m = jnp.maximum(m, jnp.max(sc, axis=1, keepdims=True))
    p = jnp.exp(s - m)
    l = jnp.sum(p, axis=1, keepdims=True)
    pc = None
    if kcb is not None:
        pc = jnp.exp(sc - m)
        l = l + jnp.sum(pc, axis=1, keepdims=True)
    return p, pc, l


def _attn_kernel(*refs, diff, rope, n_ctx, scale, out_scale):
    it = iter(refs)
    q_ref, k_ref, v_ref = next(it), next(it), next(it)
    ck_ref = cv_ref = None
    if n_ctx:
        ck_ref, cv_ref = next(it), next(it)
    if rope:
        cosq_ref, sinq_ref, cosk_ref, sink_ref = next(it), next(it), next(it), next(it)
    if diff:
        lam_ref, g_ref = next(it), next(it)
    o_ref = next(it)
    kb_ref, vb_ref = next(it), next(it)
    if n_ctx:
        kcb_ref, vcb_ref = next(it), next(it)

    @pl.when(pl.program_id(2) == 0)
    def _():
        k = k_ref[...].astype(F32)
        if rope:
            k = _rope(k, cosk_ref[...], sink_ref[...])
        kb_ref[...] = k.astype(BF16)
        _store_values(vb_ref, v_ref[...])
        if n_ctx:
            kcb_ref[...] = ck_ref[...].astype(BF16)
            _store_values(vcb_ref, cv_ref[...])

    q = q_ref[...].astype(F32)
    if rope:
        q = _rope(q, cosq_ref[...], sinq_ref[...])
    q = q * scale
    kb = kb_ref[...]
    kcb = kcb_ref[...] if n_ctx else None
    if diff:
        lane = lax.broadcasted_iota(jnp.int32, q.shape, 1)
        p0, pc0, l0 = _softmax_parts(jnp.where(lane < DA_QK_DIM, q, 0.0).astype(BF16), kb, kcb)
        p1, pc1, l1 = _softmax_parts(jnp.where(lane >= DA_QK_DIM, q, 0.0).astype(BF16), kb, kcb)
        w0 = 1.0 / l0
        w1 = lam_ref[0:1, 0:1] / l1
        o = jnp.dot((p0 * w0 - p1 * w1).astype(BF16), vb_ref[:, 0:LANES], preferred_element_type=F32)
        if n_ctx:
            o += jnp.dot((pc0 * w0 - pc1 * w1).astype(BF16), vcb_ref[:, 0:LANES], preferred_element_type=F32)
        ms = jnp.mean(o * o, axis=-1, keepdims=True)
        o = (o * lax.rsqrt(ms + NORM_EPS) * g_ref[...]) * out_scale
    else:
        n0, l0 = _softmax_values(q.astype(BF16), kb, vb_ref[...], kcb, vcb_ref[...] if n_ctx else None)
        o = n0 / l0
    o_ref[...] = o.astype(o_ref.dtype)


def attention(qkv, *, row0, n_batch, n_seq, tq, diff, ctx=None, rope_tabs=None, lam=None, subln=None,
              out_scale=1.0, scale):
    nq = n_seq // tq
    qb0, kb0 = row0 // tq, row0 // n_seq
    diff_ = diff
    rope = rope_tabs is not None
    n_ctx = 0 if ctx is None else ctx[0].shape[2]
    in_specs = [
        pl.BlockSpec((tq, LANES), lambda b, h, i: (qb0 + b * nq + i, h)),
        pl.BlockSpec((n_seq, LANES), lambda b, h, i: (kb0 + b, HEADS + h)),
        pl.BlockSpec((n_seq, LANES), lambda b, h, i: (kb0 + b, 2 * HEADS + h)),
    ]
    args = [qkv, qkv, qkv]
    scratch = [pltpu.VMEM((n_seq, LANES), BF16), pltpu.VMEM((n_seq, 2 * LANES), BF16)]
    if n_ctx:
        ck, cv, cl = ctx
        cspec = pl.BlockSpec((None, None, n_ctx, LANES), lambda b, h, i: (b, cl, 0, h))
        in_specs += [cspec, cspec]
        args += [ck, cv]
        scratch += [pltpu.VMEM((n_ctx, LANES), BF16), pltpu.VMEM((n_ctx, 2 * LANES), BF16)]
    if rope:
        cos, sin = rope_tabs
        in_specs += [pl.BlockSpec((tq, LANES), lambda b, h, i: (i, 0))] * 2
        in_specs += [pl.BlockSpec((n_seq, LANES), lambda b, h, i: (0, 0))] * 2
        args += [cos, sin, cos, sin]
    if diff_:
        in_specs += [pl.BlockSpec((1, LANES), lambda b, h, i: (0, 0))] * 2
        args += [lam, subln]
    return pl.pallas_call(
        functools.partial(_attn_kernel, diff=diff_, rope=rope, n_ctx=n_ctx, scale=scale, out_scale=out_scale),
        grid=(n_batch, HEADS, nq),
        in_specs=in_specs,
        out_specs=pl.BlockSpec((tq, LANES), lambda b, h, i: (b * nq + i, h)),
        out_shape=jax.ShapeDtypeStruct((n_batch * n_seq, D_MODEL), BF16),
        scratch_shapes=scratch,
        compiler_params=_cparams(3), name="attention",
    )(*args)


LOG2_E = 1.4426950408889634


def _diff_long_kernel(q_ref, k_ref, v_ref, ck_ref, cv_ref, cosq_ref, sinq_ref, cosk_ref, sink_ref, lam_ref, g_ref,
                      o_ref, kb_ref, vt_ref, *, n_lat, n_ctx, scale, out_scale):
    n_k = n_lat + n_ctx

    @pl.when(pl.program_id(2) == 0)
    def _():
        kb_ref[0:n_lat, :] = _rope(k_ref[...].astype(F32), cosk_ref[...], sink_ref[...]).astype(BF16)
        kb_ref[n_lat:n_k, :] = ck_ref[...].astype(BF16)
        vt_ref[:, 0:n_lat] = v_ref[...].astype(F32).T.astype(BF16)
        vt_ref[:, n_lat:n_k] = cv_ref[...].astype(F32).T.astype(BF16)

    q = _rope(q_ref[...].astype(F32), cosq_ref[...], sinq_ref[...]) * (scale * LOG2_E)
    lane = lax.broadcasted_iota(jnp.int32, q.shape, 1)
    kb = kb_ref[...]
    dn = (((1,), (1,)), ((), ()))
    parts = []
    for q_c in (jnp.where(lane < DA_QK_DIM, q, 0.0), jnp.where(lane >= DA_QK_DIM, q, 0.0)):
        st = lax.dot_general(kb, q_c.astype(BF16), dn, preferred_element_type=F32)
        p = jnp.exp2(st - jnp.max(st, axis=0, keepdims=True))
        parts.append((p, jnp.sum(p, axis=0, keepdims=True)))
    (p0, l0), (p1, l1) = parts
    a_t = (p0 - p1 * (lam_ref[0:1, 0:1] * l0 / l1)).astype(BF16)
    o_t = jnp.dot(vt_ref[...], a_t, preferred_element_type=F32) * (1.0 / l0)
    o = o_t.T
    ms = jnp.mean(o * o, axis=-1, keepdims=True)
    o_ref[...] = ((o * lax.rsqrt(ms + NORM_EPS) * g_ref[...]) * out_scale).astype(o_ref.dtype)


def diff_attention_long(qkv, *, row0, n_batch, n_seq, tq, ctx, rope_tabs, lam, subln, out_scale, scale):
    nq = n_seq // tq
    qb0, kb0 = row0 // tq, row0 // n_seq
    ck, cv, cl = ctx
    n_ctx = ck.shape[2]
    cos, sin = rope_tabs
    cspec = pl.BlockSpec((None, None, n_ctx, LANES), lambda b, h, i: (b, cl, 0, h))
    qtab = pl.BlockSpec((tq, LANES), lambda b, h, i: (i, 0))
    ktab = pl.BlockSpec((n_seq, LANES), lambda b, h, i: (0, 0))
    row = pl.BlockSpec((1, LANES), lambda b, h, i: (0, 0))
    return pl.pallas_call(
        functools.partial(_diff_long_kernel, n_lat=n_seq, n_ctx=n_ctx, scale=scale, out_scale=out_scale),
        grid=(n_batch, HEADS, nq),
        in_specs=[pl.BlockSpec((tq, LANES), lambda b, h, i: (qb0 + b * nq + i, h)),
                  pl.BlockSpec((n_seq, LANES), lambda b, h, i: (kb0 + b, HEADS + h)),
                  pl.BlockSpec((n_seq, LANES), lambda b, h, i: (kb0 + b, 2 * HEADS + h)),
                  cspec, cspec, qtab, qtab, ktab, ktab, row, row],
        out_specs=pl.BlockSpec((tq, LANES), lambda b, h, i: (b * nq + i, h)),
        out_shape=jax.ShapeDtypeStruct((n_batch * n_seq, D_MODEL), BF16),
        scratch_shapes=[pltpu.VMEM((n_seq + n_ctx, LANES), BF16), pltpu.VMEM((LANES, n_seq + n_ctx), BF16)],
        compiler_params=_cparams(3), name="diff_attention_long",
    )(qkv, qkv, qkv, ck, cv, cos, sin, cos, sin, lam, subln)


def _na_key_row0(t):
    return jnp.clip(t * NA_TILE_ROWS - NA_WIN_ROWS // 2, 0, DEC_SEQ // GRID_W - NA_KEY_ROWS)


def _na_kernel(q_ref, k_ref, v_ref, ck_ref, cv_ref, bias_ref, o_ref, kb_ref, vb_ref, kcb_ref, vcb_ref, *, scale):
    t = pl.program_id(2)

    @pl.when(t == 0)
    def _():
        kb_ref[...] = k_ref[...].astype(BF16)
        _store_values(vb_ref, v_ref[...])
        kcb_ref[...] = ck_ref[...].astype(BF16)
        _store_values(vcb_ref, cv_ref[...])

    n_keys = NA_KEY_ROWS * GRID_W
    k0 = pl.multiple_of(_na_key_row0(t) * GRID_W, GRID_W)
    kw = kb_ref[pl.ds(k0, n_keys), :]
    vw = vb_ref[pl.ds(k0, n_keys), :]
    q = (q_ref[...].astype(F32) * scale).astype(BF16)
    n0, l0 = _softmax_values(q, kw, vw, kcb_ref[...], vcb_ref[...], bias=bias_ref[...])
    o_ref[...] = (n0 / l0).astype(o_ref.dtype)


def na_bias_table(rpb):
    rows = DEC_SEQ // GRID_W
    n_h = rpb.shape[0]
    col = jnp.arange(GRID_W)
    col0 = jnp.clip(col - NA_WIN_COLS // 2, 0, GRID_W - NA_WIN_COLS)
    col_ok = (col[None, :] >= col0[:, None]) & (col[None, :] < col0[:, None] + NA_WIN_COLS)
    dc_idx = jnp.clip(col[None, :] - col[:, None], -(NA_WIN_COLS - 1), NA_WIN_COLS - 1) + NA_WIN_COLS - 1
    pick = (dc_idx[None] == jnp.arange(2 * NA_WIN_COLS - 1)[:, None, None]).astype(F32)
    band = jnp.einsum('hrc,cqk->hrqk', rpb.astype(F32), pick, precision=lax.Precision.HIGHEST)
    band = jnp.where(col_ok, band, -jnp.inf)
    masked = jnp.full((n_h, GRID_W, GRID_W), -jnp.inf, F32)
    classes = []
    for tile in (0, 1, rows // NA_TILE_ROWS - 1):
        u0 = min(max(tile * NA_TILE_ROWS - NA_WIN_ROWS // 2, 0), rows - NA_KEY_ROWS)
        q_rows = []
        for jq in range(NA_TILE_ROWS):
            r = tile * NA_TILE_ROWS + jq
            r0 = min(max(r - NA_WIN_ROWS // 2, 0), rows - NA_WIN_ROWS)
            blocks = [band[:, u0 + u - r + NA_WIN_ROWS - 1] if r0 <= u0 + u < r0 + NA_WIN_ROWS else masked
                      for u in range(NA_KEY_ROWS)]
            q_rows.append(jnp.concatenate(blocks, axis=-1))
        classes.append(jnp.concatenate(q_rows, axis=1))
    return jnp.stack(classes, axis=1)


def na_attention(qkv, row0, cache_k, cache_v, cl, bias, scale):
    tq = NA_TILE_ROWS * GRID_W
    nt = DEC_SEQ // tq
    qb0, kb0 = row0 // tq, row0 // DEC_SEQ
    n_ctx = cache_k.shape[2]
    cspec = pl.BlockSpec((None, None, n_ctx, LANES), lambda b, h, t: (b, cl, 0, h))
    scratch = [pltpu.VMEM((DEC_SEQ, LANES), BF16), pltpu.VMEM((DEC_SEQ, 2 * LANES), BF16),
               pltpu.VMEM((n_ctx, LANES), BF16), pltpu.VMEM((n_ctx, 2 * LANES), BF16)]
    return pl.pallas_call(
        functools.partial(_na_kernel, scale=scale),
        grid=(DEC_BATCH, HEADS, nt),
        in_specs=[
            pl.BlockSpec((tq, LANES), lambda b, h, t: (qb0 + b * nt + t, h)),
            pl.BlockSpec((DEC_SEQ, LANES), lambda b, h, t: (kb0 + b, HEADS + h)),
            pl.BlockSpec((DEC_SEQ, LANES), lambda b, h, t: (kb0 + b, 2 * HEADS + h)),
            cspec, cspec,
            pl.BlockSpec((None, None, tq, NA_KEY_ROWS * GRID_W),
                         lambda b, h, t: (h, jnp.minimum(t, 1) + t // (nt - 1), 0, 0)),
        ],
        out_specs=pl.BlockSpec((tq, LANES), lambda b, h, t: (b * nt + t, h)),
        out_shape=jax.ShapeDtypeStruct((T_SAMPLE, D_MODEL), BF16),
        scratch_shapes=scratch,
        compiler_params=_cparams(3), name="na_attention",
    )(qkv, qkv, qkv, cache_k, cache_v, bias)


HG_LEVELS = HG_TILE.bit_length() - 1


def hgrn_masks():
    t = jnp.arange(HG_TILE)
    pair, qsel = [], []
    for d in range(2):
        for lvl in range(HG_LEVELS):
            upper = ((t >> lvl) & 1) == 1
            is_q = jnp.logical_not(upper) if d else upper
            same = (t[:, None] >> (lvl + 1)) == (t[None, :] >> (lvl + 1))
            pair.append(same & is_q[:, None] & jnp.logical_not(is_q)[None, :])
            qsel.append(jnp.broadcast_to(is_q[:, None], (HG_TILE, HG_TILE)))
    shape = (2, HG_LEVELS, HG_TILE, HG_TILE)
    return jnp.stack(pair).astype(F32).reshape(shape), jnp.stack(qsel).astype(F32).reshape(shape)


def _hgrn_tile(q, kk, f, v_f32, st_bf, d, pair_ref, qsel_ref):
    c = q.shape[0]
    rev = d == 1
    dn = (((1,), (1,)), ((), ()))
    px = f
    py = jnp.ones_like(f)
    pt = f
    att = jnp.zeros((c, c), F32)
    for lvl in range(HG_LEVELS):
        m = 1 << lvl
        qp = (q * px).astype(BF16)
        kp = (kk * py).astype(BF16)
        att = att + lax.dot_general(qp, kp, dn, preferred_element_type=F32) * pair_ref[d, lvl]
        below = pltpu.roll(pt, m, 0)
        above = pltpu.roll(pt, c - m, 0)
        is_q = qsel_ref[d, lvl] > 0.5
        sib = jnp.where(is_q, above, below) if rev else jnp.where(is_q, below, above)
        grown = jnp.where(is_q, px, py) * sib
        px = jnp.where(is_q, grown, px)
        py = jnp.where(is_q, py, grown)
        pt = pt * sib
    tot = pt[0:1, :]
    v_bf = v_f32.astype(BF16)
    diag = jnp.sum(q * kk, axis=1, keepdims=True)
    o = jnp.dot(att.astype(BF16), v_bf, preferred_element_type=F32) + diag * v_f32
    o += lax.dot_general((q * px).astype(BF16), st_bf, dn, preferred_element_type=F32)
    inc = jnp.dot(v_f32.T.astype(BF16), (kk * py).astype(BF16), preferred_element_type=F32)
    return o, inc, tot


def _hgrn_kernel(*refs, n, has_s0, emit_state):
    it = iter(refs)
    q_ref, i_ref, zf_ref, zb_ref, g_ref = next(it), next(it), next(it), next(it), next(it)
    s0_ref = next(it) if has_s0 else None
    lb_ref, gn_ref, pair_ref, qsel_ref = next(it), next(it), next(it), next(it)
    o_ref = next(it)
    sfin_ref = next(it) if emit_state else None
    of_ref, ob_ref, st_ref = next(it), next(it), next(it)
    nt = n // HG_TILE
    lb, one_m_lb = lb_ref[0:1, :], lb_ref[1:2, :]

    def gates(z):
        e = jnp.exp(-jnp.abs(z))
        r = 1.0 / (1.0 + e)
        pos = z > 0
        f = lb + one_m_lb * (jnp.where(pos, 1.0, e) * r)
        kk = one_m_lb * (jnp.where(pos, e, 1.0) * r)
        return f, kk

    for d in range(2):
        if has_s0:
            st_ref[d] = s0_ref[d].T
        else:
            st_ref[d] = jnp.zeros((HEAD_DIM, HEAD_DIM), F32)

    def scan_step(step, carry):
        for d, (z_ref, out_ref) in enumerate(((zf_ref, of_ref), (zb_ref, ob_ref))):
            ti = (nt - 1 - step) if d else step
            rows = pl.ds(pl.multiple_of(ti * HG_TILE, HG_TILE), HG_TILE)
            f, kk = gates(z_ref[rows, :])
            qr = q_ref[rows, :].astype(F32)
            q = (qr / (1.0 + jnp.exp(-qr))) * (HEAD_DIM ** -0.5)
            st = st_ref[d]
            o, inc, tot = _hgrn_tile(q, kk, f, i_ref[rows, :].astype(F32), st.astype(BF16), d,
                                     pair_ref, qsel_ref)
            st_ref[d] = st * tot + inc
            out_ref[rows, :] = o
        return carry

    lax.fori_loop(0, nt, scan_step, 0)

    def finish(ti, carry):
        rows = pl.ds(pl.multiple_of(ti * HG_TILE, HG_TILE), HG_TILE)
        o = of_ref[rows, :] + ob_ref[rows, :]
        ms = jnp.mean(o * o, axis=-1, keepdims=True)
        o = o * lax.rsqrt(ms + NORM_EPS) * gn_ref[...]
        g = g_ref[rows, :].astype(F32)
        o_ref[rows, :] = (o * (g / (1.0 + jnp.exp(-g)))).astype(o_ref.dtype)
        return carry

    lax.fori_loop(0, nt, finish, 0)
    if emit_state:
        for d in range(2):
            sfin_ref[d] = st_ref[d].T


def hgrn_scan(qi, z, g, lbp, layer, gnorm, masks, *, row0, n_batch, n_seq, s0=None, s0_layer=0, emit_state=False):
    rb0 = row0 // n_seq
    blk = lambda c0: pl.BlockSpec((n_seq, LANES), lambda b, h: (rb0 + b, c0 + h))
    in_specs = [blk(0), blk(HEADS), blk(0), blk(HEADS), blk(0)]
    args = [qi, qi, z, z, g]
    if s0 is not None:
        in_specs.append(pl.BlockSpec((None, None, 2, None, HEAD_DIM, HEAD_DIM),
                                     lambda b, h: (b, s0_layer, 0, h, 0, 0)))
        args.append(s0)
    mask_spec = pl.BlockSpec((2, HG_LEVELS, HG_TILE, HG_TILE), lambda b, h: (0, 0, 0, 0))
    in_specs += [pl.BlockSpec((None, MOD_ROWS, LANES), lambda b, h: (layer, 0, h)),
                 pl.BlockSpec((1, LANES), lambda b, h: (0, 0)), mask_spec, mask_spec]
    args += [lbp, gnorm, masks[0], masks[1]]
    out_specs = [pl.BlockSpec((n_seq, LANES), lambda b, h: (b, h))]
    out_shape = [jax.ShapeDtypeStruct((n_batch * n_seq, D_MODEL), BF16)]
    if emit_state:
        out_specs.append(pl.BlockSpec((None, 2, None, HEAD_DIM, HEAD_DIM), lambda b, h: (b, 0, h, 0, 0)))
        out_shape.append(jax.ShapeDtypeStruct((n_batch, 2, HEADS, HEAD_DIM, HEAD_DIM), F32))
    res = pl.pallas_call(
        functools.partial(_hgrn_kernel, n=n_seq, has_s0=s0 is not None, emit_state=emit_state),
        grid=(n_batch, HEADS),
        in_specs=in_specs, out_specs=out_specs, out_shape=out_shape,
        scratch_shapes=[pltpu.VMEM((n_seq, LANES), F32), pltpu.VMEM((n_seq, LANES), F32),
                        pltpu.VMEM((2, HEAD_DIM, HEAD_DIM), F32)],
        compiler_params=_cparams(2), name="hgrn_scan",
    )(*args)
    return res if emit_state else res[0]


def _expert_kernel(be_ref, first_ref, nused_ref, x_ref, wg_ref, wu_ref, wd_ref, o_ref, wg_bf, wu_bf, wd_bf):
    g = pl.program_id(0)

    @pl.when(g < nused_ref[0])
    def _():
        @pl.when(first_ref[g] == 1)
        def _():
            wg_bf[...] = wg_ref[...].astype(BF16)
            wu_bf[...] = wu_ref[...].astype(BF16)
            wd_bf[...] = wd_ref[...].astype(BF16)

        x = x_ref[...].astype(BF16)
        a = jnp.dot(x, wg_bf[...], preferred_element_type=F32)
        u = jnp.dot(x, wu_bf[...], preferred_element_type=F32)
        hmid = (a / (1.0 + jnp.exp(-a))) * u
        o_ref[...] = jnp.dot(hmid.astype(BF16), wd_bf[...], preferred_element_type=F32).astype(o_ref.dtype)

    @pl.when(g >= nused_ref[0])
    def _():
        o_ref[...] = jnp.zeros_like(o_ref)


def experts(xb, block_e, first, n_used, w_gate, w_up, w_down, layer):
    cap, d = xb.shape
    nb = cap // MOE_ROWS
    de = w_gate.shape[-1]
    grid_spec = pltpu.PrefetchScalarGridSpec(
        num_scalar_prefetch=3,
        grid=(nb,),
        in_specs=[
            pl.BlockSpec((MOE_ROWS, d), lambda g, be, fi, nu: (g, 0)),
            pl.BlockSpec((None, None, d, de), lambda g, be, fi, nu: (layer, be[g], 0, 0)),
            pl.BlockSpec((None, None, d, de), lambda g, be, fi, nu: (layer, be[g], 0, 0)),
            pl.BlockSpec((None, None, de, d), lambda g, be, fi, nu: (layer, be[g], 0, 0)),
        ],
        out_specs=pl.BlockSpec((MOE_ROWS, d), lambda g, be, fi, nu: (g, 0)),
        scratch_shapes=[pltpu.VMEM((d, de), BF16), pltpu.VMEM((d, de), BF16), pltpu.VMEM((de, d), BF16)],
    )
    return pl.pallas_call(
        _expert_kernel, grid_spec=grid_spec,
        out_shape=jax.ShapeDtypeStruct((cap, d), F32),
        compiler_params=_cparams(1), name="experts",
    )(block_e, first, n_used, xb, w_gate, w_up, w_down)


def _dispatch_kernel(pos_ref, segend_ref, cnt_ref, nused_ref, h_ref, xb_hbm, zero_ref, sem, zsem):
    i = pl.program_id(0)
    rows = h_ref.shape[0]
    n_blocks = xb_hbm.shape[0] // MOE_ROWS

    def zero_block(start):
        return pltpu.make_async_copy(zero_ref, xb_hbm.at[pl.ds(pl.multiple_of(start, MOE_ROWS), MOE_ROWS), :], zsem)

    @pl.when(i == 0)
    def _():
        zero_ref[...] = jnp.zeros_like(zero_ref)

        def start(e, c):
            @pl.when(cnt_ref[e] > 0)
            def _():
                zero_block(segend_ref[e] - MOE_ROWS).start()
            return c

        def wait(e, c):
            @pl.when(cnt_ref[e] > 0)
            def _():
                zero_block(segend_ref[e] - MOE_ROWS).wait()
            return c

        def start_tail(g, c):
            zero_block(g * MOE_ROWS).start()
            return c

        def wait_tail(g, c):
            zero_block(g * MOE_ROWS).wait()
            return c

        lax.fori_loop(0, MOE_EXPERTS, start, 0)
        lax.fori_loop(nused_ref[0], n_blocks, start_tail, 0)
        lax.fori_loop(0, MOE_EXPERTS, wait, 0)
        lax.fori_loop(nused_ref[0], n_blocks, wait_tail, 0)

    def issue(r, c):
        src = h_ref.at[pl.ds(r, 1), :]
        pltpu.make_async_copy(src, xb_hbm.at[pl.ds(pos_ref[2 * r], 1), :], sem).start()
        pltpu.make_async_copy(src, xb_hbm.at[pl.ds(pos_ref[2 * r + 1], 1), :], sem).start()
        return c

    lax.fori_loop(0, rows, issue, 0, unroll=8)
    pltpu.make_async_copy(h_ref, xb_hbm.at[pl.ds(0, rows), :], sem).wait()
    pltpu.make_async_copy(h_ref, xb_hbm.at[pl.ds(0, rows), :], sem).wait()


def dispatch(h, pos_flat, seg_end, counts, n_used, cap):
    t, d = h.shape
    rows = 256
    grid_spec = pltpu.PrefetchScalarGridSpec(
        num_scalar_prefetch=0,
        grid=(t // rows,),
        in_specs=[pl.BlockSpec((2 * rows,), lambda i: (i,), memory_space=pltpu.SMEM),
                  pl.BlockSpec(memory_space=pltpu.SMEM),
                  pl.BlockSpec(memory_space=pltpu.SMEM),
                  pl.BlockSpec(memory_space=pltpu.SMEM),
                  pl.BlockSpec((rows, d), lambda i: (i, 0))],
        out_specs=pl.BlockSpec(memory_space=pl.ANY),
        scratch_shapes=[pltpu.VMEM((MOE_ROWS, d), F32), pltpu.SemaphoreType.DMA(()), pltpu.SemaphoreType.DMA(())],
    )
    return pl.pallas_call(
        _dispatch_kernel, grid_spec=grid_spec,
        out_shape=jax.ShapeDtypeStruct((cap, d), F32),
        compiler_params=_cparams(1), name="dispatch",
    )(pos_flat, seg_end, counts, n_used, h)


def _combine_kernel(pos_ref, x_ref, r_ref, mod_ref, gain_ref, modn_ref, yb_hbm, *rest, g_idx, final):
    if final:
        y_ref, buf0, buf1, sem = rest
    else:
        o_ref, h_ref, buf0, buf1, sem = rest
    tm = x_ref.shape[0]

    def issue(r, c):
        pltpu.make_async_copy(yb_hbm.at[pl.ds(pos_ref[2 * r], 1), :], buf0.at[pl.ds(r, 1), :], sem).start()
        pltpu.make_async_copy(yb_hbm.at[pl.ds(pos_ref[2 * r + 1], 1), :], buf1.at[pl.ds(r, 1), :], sem).start()
        return c

    lax.fori_loop(0, tm, issue, 0, unroll=8)
    pltpu.make_async_copy(yb_hbm.at[pl.ds(0, tm), :], buf0, sem).wait()
    pltpu.make_async_copy(yb_hbm.at[pl.ds(0, tm), :], buf1, sem).wait()
    r = r_ref[...]
    y = r[:, 0:1] * buf0[...] + r[:, 1:2] * buf1[...]
    x = x_ref[...] + mod_ref[g_idx:g_idx + 1, :] * y
    if final:
        ms = jnp.mean(x * x, axis=-1, keepdims=True)
        y_ref[...] = x * lax.rsqrt(ms + NORM_EPS) * gain_ref[...]
    else:
        o_ref[...] = x
        h_ref[...] = _norm_mod(x, gain_ref[...], modn_ref[...], 0, 1).astype(BF16)


def combine(x, yb, pos_flat, route, modseg, g_idx, gains, gain_idx, modseg_next, final):
    t, d = x.shape
    tm = 256
    row_spec = pl.BlockSpec((tm, d), lambda i: (i, 0))
    mod_spec = pl.BlockSpec((None, MOD_ROWS, d), lambda i: (i * tm // SEG, 0, 0))
    if final:
        out_specs, out_shape = row_spec, jax.ShapeDtypeStruct((t, d), F32)
    else:
        out_specs = [row_spec, row_spec]
        out_shape = [jax.ShapeDtypeStruct((t, d), F32), jax.ShapeDtypeStruct((t, d), BF16)]
    grid_spec = pltpu.PrefetchScalarGridSpec(
        num_scalar_prefetch=0,
        grid=(t // tm,),
        in_specs=[pl.BlockSpec((2 * tm,), lambda i: (i,), memory_space=pltpu.SMEM),
                  row_spec,
                  pl.BlockSpec((tm, LANES), lambda i: (i, 0)),
                  mod_spec,
                  pl.BlockSpec((None, 1, d), lambda i: (gain_idx, 0, 0)),
                  mod_spec,
                  pl.BlockSpec(memory_space=pl.ANY)],
        out_specs=out_specs,
        scratch_shapes=[pltpu.VMEM((tm, d), F32), pltpu.VMEM((tm, d), F32), pltpu.SemaphoreType.DMA(())],
    )
    return pl.pallas_call(
        functools.partial(_combine_kernel, g_idx=g_idx, final=final), grid_spec=grid_spec, out_shape=out_shape,
        compiler_params=_cparams(1), name="combine",
    )(pos_flat, x, route, modseg, gains.reshape(gains.shape[0], 1, d), modseg_next, yb)


def moe_plan(route, cnt):
    t = route.shape[0]
    counts = cnt[0, :MOE_EXPERTS].astype(jnp.int32)
    padded = (counts + MOE_ROWS - 1) // MOE_ROWS * MOE_ROWS
    seg_end = jnp.cumsum(padded).astype(jnp.int32)
    seg_start = seg_end - padded
    e = route[:, 2:4].astype(jnp.int32)
    rank = route[:, 4:6].astype(jnp.int32)
    onehot = e[:, :, None] == jnp.arange(MOE_EXPERTS, dtype=jnp.int32)
    pos = rank + jnp.sum(jnp.where(onehot, seg_start, 0), axis=-1)
    n_blocks = -(-(2 * t + MOE_EXPERTS * (MOE_ROWS - 1)) // MOE_ROWS)
    blk0 = jnp.arange(n_blocks, dtype=jnp.int32) * MOE_ROWS
    block_e = jnp.minimum(jnp.sum(seg_end[None, :] <= blk0[:, None], axis=1), MOE_EXPERTS - 1).astype(jnp.int32)
    first = jnp.concatenate([jnp.ones((1,), jnp.int32), (block_e[1:] != block_e[:-1]).astype(jnp.int32)])
    n_used = (seg_end[-1] // MOE_ROWS).astype(jnp.int32).reshape(1)
    return pos.reshape(-1).astype(jnp.int32), seg_end, counts, block_e, first, n_used, n_blocks * MOE_ROWS


def _rope_tables(n, dim):
    quarter = dim // 4
    inv = jnp.power(ROPE_BASE, -jnp.arange(quarter, dtype=F32) / quarter)
    t = jnp.arange(n)
    row = (t // GRID_W).astype(F32)
    col = (t % GRID_W).astype(F32)
    ang = jnp.concatenate([row[:, None] * inv, col[:, None] * inv], axis=-1)
    cos, sin = jnp.cos(ang), jnp.sin(ang)
    reps = LANES // dim
    cos_l = jnp.tile(jnp.concatenate([cos, cos], axis=-1), (1, reps))
    sin_l = jnp.tile(jnp.concatenate([-sin, sin], axis=-1), (1, reps))
    return cos_l, sin_l


def kernel(x_prompt, x_sample, state_hgrn, cache_diff_k, cache_diff_v, cache_na_k, cache_na_v, c, c_ctx, norm_mix, norm_ffn, w_ada, b_ada, router_group_w, router_group_b, router_expert_w, router_expert_b, expert_w_gate, expert_w_up, expert_w_down, hg_w_in, hg_lb_logits, hg_gnorm, hg_w_out, da_w_qkv, da_lambda, da_subln, da_w_out, na_w_qkv, na_rpb, na_w_out, final_norm):
    d = D_MODEL
    x = jnp.concatenate([x_prompt.reshape(T_PROMPT, d), x_sample.reshape(T_SAMPLE, d)], axis=0)

    cond8 = jnp.zeros((MOD_ROWS, d), F32).at[0].set(c_ctx).at[1:1 + DEC_BATCH].set(c)
    mod_all = ada_all(cond8, w_ada, b_ada)
    seg_cond = jnp.array([0] * (T_PROMPT // SEG) + list(range(1, 1 + DEC_BATCH)), jnp.int32)
    mod_all = mod_all.reshape(DEPTH, MOD_ROWS, N_MOD, d)[:, seg_cond]
    mod_all = jnp.pad(mod_all, ((0, 0), (0, 0), (0, MOD_ROWS - N_MOD), (0, 0)))

    p = jax.nn.softmax(hg_lb_logits.astype(F32), axis=0)
    cum = jnp.cumsum(p, axis=0)
    lb = cum - cum[0:1]
    lbp = jnp.stack([lb, 1.0 - lb], axis=1)
    lbp = jnp.pad(lbp, ((0, 0), (0, MOD_ROWS - 2), (0, 0)))

    wr = jnp.concatenate([router_group_w, router_expert_w], axis=-1)
    wr = jnp.pad(wr, ((0, 0), (0, 0), (0, LANES - wr.shape[-1])))
    wr_hi = wr.astype(BF16)
    wr_lo = (wr - wr_hi.astype(F32)).astype(BF16)
    br = jnp.concatenate([router_group_b, router_expert_b], axis=-1).astype(F32)
    br = jnp.pad(br, ((0, 0), (0, LANES - br.shape[-1])))[:, None, :]

    hg_masks = hgrn_masks()
    cos_l, sin_l = _rope_tables(DEC_SEQ, DA_QK_DIM)
    cdk = cache_diff_k.reshape(DEC_BATCH, -1, PAST_LEN, d)
    cdv = cache_diff_v.reshape(DEC_BATCH, -1, PAST_LEN, d)
    cnk = cache_na_k.reshape(DEC_BATCH, -1, PAST_LEN, d)
    cnv = cache_na_v.reshape(DEC_BATCH, -1, PAST_LEN, d)

    new_hg, new_dk, new_dv, new_nk, new_nv = [], [], [], [], []
    h = norm_mod(x, norm_mix, 0, mod_all[0], 0, 1)
    for i in range(DEPTH):
        kind, j = i % N_MIXERS, i // N_MIXERS
        modseg = mod_all[i]
        if kind == 0:
            qi = matmul(h, hg_w_in, j, 0, 2 * d, BF16)
            z = matmul(h, hg_w_in, j, 2 * d, 2 * d, F32)
            g = matmul(h, hg_w_in, j, 4 * d, d, BF16)
            gn = hg_gnorm[j].reshape(1, HEAD_DIM)
            op, s_fin = hgrn_scan(qi, z, g, lbp, i, gn, hg_masks, row0=0, n_batch=BATCH, n_seq=SEQ, emit_state=True)
            os_ = hgrn_scan(qi, z, g, lbp, i, gn, hg_masks, row0=T_PROMPT, n_batch=DEC_BATCH, n_seq=DEC_SEQ,
                            s0=state_hgrn, s0_layer=j)
            new_hg.append(s_fin)
            o = jnp.concatenate([op, os_], axis=0)
            w_out = hg_w_out
        elif kind == 1:
            lam_init = 0.8 - 0.6 * math.exp(-0.3 * i)
            lp = da_lambda[j].astype(F32)
            lam = jnp.exp(jnp.sum(lp[0] * lp[1])) - jnp.exp(jnp.sum(lp[2] * lp[3])) + lam_init
            lam_row = jnp.full((1, LANES), lam, F32)
            subln = da_subln[j].reshape(1, HEAD_DIM).astype(F32)
            qkv = matmul(h, da_w_qkv, j, 0, 3 * d, F32)
            scale = DA_QK_DIM ** -0.5
            op = attention(qkv, row0=0, n_batch=BATCH, n_seq=SEQ, tq=SEQ, diff=True, lam=lam_row, subln=subln,
                           out_scale=1.0 - lam_init, scale=scale)
            os_ = diff_attention_long(qkv, row0=T_PROMPT, n_batch=DEC_BATCH, n_seq=DEC_SEQ, tq=256,
                                      ctx=(cdk, cdv, j), rope_tabs=(cos_l, sin_l), lam=lam_row, subln=subln,
                                      out_scale=1.0 - lam_init, scale=scale)
            new_dk.append(qkv[:T_PROMPT, d:2 * d].reshape(BATCH, SEQ, HEADS, 2, DA_QK_DIM))
            new_dv.append(qkv[:T_PROMPT, 2 * d:].reshape(BATCH, SEQ, HEADS, HEAD_DIM))
            o = jnp.concatenate([op, os_], axis=0)
            w_out = da_w_out
        else:
            qkv = matmul(h, na_w_qkv, j, 0, 3 * d, F32)
            scale = HEAD_DIM ** -0.5
            op = attention(qkv, row0=0, n_batch=BATCH, n_seq=SEQ, tq=SEQ, diff=False, scale=scale)
            os_ = na_attention(qkv, T_PROMPT, cnk, cnv, j, na_bias_table(na_rpb[j]), scale)
            new_nk.append(qkv[:T_PROMPT, d:2 * d].reshape(BATCH, SEQ, HEADS, HEAD_DIM))
            new_nv.append(qkv[:T_PROMPT, 2 * d:].reshape(BATCH, SEQ, HEADS, HEAD_DIM))
            o = jnp.concatenate([op, os_], axis=0)
            w_out = na_w_out
        x = matmul_resid(o, w_out, j, x, modseg, 2)

        hf, route, cnt = norm_mod(x, norm_ffn, i, modseg, 3, 4, router=(wr_hi[i], wr_lo[i], br[i]))
        pos, seg_end, counts, block_e, first, n_used, cap = moe_plan(route, cnt)
        xb = dispatch(hf, pos, seg_end, counts, n_used, cap)
        yb = experts(xb, block_e, first, n_used, expert_w_gate, expert_w_up, expert_w_down, i)
        if i + 1 < DEPTH:
            x, h = combine(x, yb, pos, route, modseg, 5, norm_mix, i + 1, mod_all[i + 1], final=False)
        else:
            y = combine(x, yb, pos, route, modseg, 5, final_norm.reshape(1, d), 0, modseg, final=True)

    return (y[:T_PROMPT].reshape(BATCH, SEQ, d), y[T_PROMPT:].reshape(DEC_BATCH, DEC_SEQ, d),
            jnp.stack(new_hg, axis=1), jnp.stack(new_dk, axis=1), jnp.stack(new_dv, axis=1),
            jnp.stack(new_nk, axis=1), jnp.stack(new_nv, axis=1))
```

```python
import functools
import math

import jax
import jax.numpy as jnp
from jax import lax
from jax.experimental import pallas as pl
from jax.experimental.pallas import tpu as pltpu

D_MODEL = 2048
BATCH = 32
SEQ = 256
DEPTH = 4
DEC_BATCH = 4
DEC_SEQ = 4096
PAST_LEN = 512
GRID_W = 64
N_MIXERS = 3
HEADS = 16
HEAD_DIM = D_MODEL // HEADS
DA_QK_DIM = 64
NA_WIN_ROWS = 8
NA_WIN_COLS = 16
MOE_GROUPS = 4
MOE_EXPERTS_PER_GROUP = 8
MOE_EXPERTS = MOE_GROUPS * MOE_EXPERTS_PER_GROUP
MOE_D_EXPERT = 512
ROPE_BASE = 10000.0
NORM_EPS = 1e-6
N_MOD = 6

T_PROMPT = BATCH * SEQ
T_SAMPLE = DEC_BATCH * DEC_SEQ
T_ALL = T_PROMPT + T_SAMPLE
SEG = DEC_SEQ
N_SEG = T_ALL // SEG
MOD_ROWS = 8

LANES = 128
VMEM_LIMIT = 56 * 1024 * 1024

TM = 512
TN = 1024
MOE_ROWS = 512
HG_TILE = 128
HG_HEADS_LONG = 2
HG_HEADS_SHORT = 4
NA_TILE_ROWS = 8
NA_KEY_ROWS = 16

F32 = jnp.float32
BF16 = jnp.bfloat16


def _cparams(n_axes):
    return pltpu.CompilerParams(dimension_semantics=("arbitrary",) * n_axes, vmem_limit_bytes=VMEM_LIMIT)


def _split_bf16(x):
    hi = x.astype(BF16)
    lo = (x - hi.astype(F32)).astype(BF16)
    return hi, lo


def _pack_rows(x):
    n = x.shape[1] // 2
    hi = lax.bitcast_convert_type(x[:, 0:n].astype(BF16).astype(F32), jnp.uint32)
    lo = lax.bitcast_convert_type(x[:, n:2 * n].astype(BF16).astype(F32), jnp.uint32)
    return hi | (lo >> 16)


def _unpack_rows(u):
    hi = lax.bitcast_convert_type(u & jnp.uint32(0xFFFF0000), F32)
    lo = lax.bitcast_convert_type(u << 16, F32)
    return jnp.concatenate([hi, lo], axis=1)


def _ada_kernel(cond_ref, w_ref, b_ref, o_ref):
    a = cond_ref[...]
    a = a / (1.0 + jnp.exp(-a))
    a_hi, a_lo = _split_bf16(a)
    w_hi, w_lo = _split_bf16(w_ref[...])
    acc = jnp.dot(a_hi, w_hi, preferred_element_type=F32)
    acc += jnp.dot(a_lo, w_hi, preferred_element_type=F32)
    acc += jnp.dot(a_hi, w_lo, preferred_element_type=F32)
    o_ref[...] = acc + b_ref[...]


def ada_all(cond8, w_ada, b_ada):
    depth, d, n = w_ada.shape
    tn = 1024
    return pl.pallas_call(
        _ada_kernel,
        grid=(depth, n // tn),
        in_specs=[
            pl.BlockSpec((MOD_ROWS, d), lambda l, j: (0, 0)),
            pl.BlockSpec((None, d, tn), lambda l, j: (l, 0, j)),
            pl.BlockSpec((None, 1, tn), lambda l, j: (l, 0, j)),
        ],
        out_specs=pl.BlockSpec((None, MOD_ROWS, tn), lambda l, j: (l, 0, j)),
        out_shape=jax.ShapeDtypeStruct((depth, MOD_ROWS, n), F32),
        compiler_params=_cparams(2),
        name="ada",
    )(cond8, w_ada, b_ada.reshape(depth, 1, n))


def _norm_mod(x, gain, mod, sh_idx, sc_idx):
    ms = jnp.mean(x * x, axis=-1, keepdims=True)
    y = x * lax.rsqrt(ms + NORM_EPS) * gain
    return y * (1.0 + mod[sc_idx:sc_idx + 1, :]) + mod[sh_idx:sh_idx + 1, :]


def _norm_kernel(x_ref, gain_ref, mod_ref, h_ref, *, sh_idx, sc_idx):
    h_ref[...] = _norm_mod(x_ref[...], gain_ref[...], mod_ref[...], sh_idx, sc_idx).astype(BF16)


def _route(logits):
    lane = lax.broadcasted_iota(jnp.int32, logits.shape, 1)
    neg = jnp.float32(-jnp.inf)
    big = jnp.int32(LANES)
    lg = jnp.where(lane < MOE_GROUPS, logits, neg)
    mg = jnp.max(lg, axis=1, keepdims=True)
    grp = jnp.min(jnp.where(lg == mg, lane, big), axis=1, keepdims=True)
    p_grp = 1.0 / jnp.sum(jnp.exp(lg - mg), axis=1, keepdims=True)
    lo = MOE_GROUPS + grp * MOE_EXPERTS_PER_GROUP
    le = jnp.where((lane >= lo) & (lane < lo + MOE_EXPERTS_PER_GROUP), logits, neg)
    m1 = jnp.max(le, axis=1, keepdims=True)
    i1 = jnp.min(jnp.where(le == m1, lane, big), axis=1, keepdims=True)
    le2 = jnp.where(lane == i1, neg, le)
    m2 = jnp.max(le2, axis=1, keepdims=True)
    i2 = jnp.min(jnp.where(le2 == m2, lane, big), axis=1, keepdims=True)
    p2 = jnp.exp(m2 - m1)
    g1 = p_grp / (1.0 + p2)
    g2 = p_grp * p2 / (1.0 + p2)
    return g1, g2, i1 - MOE_GROUPS, i2 - MOE_GROUPS


def _norm_route_kernel(x_ref, gain_ref, mod_ref, wr_hi_ref, wr_lo_ref, br_ref, h_ref, r_ref, cnt_ref, run_ref,
                       *, sh_idx, sc_idx):
    @pl.when(pl.program_id(0) == 0)
    def _():
        run_ref[...] = jnp.zeros_like(run_ref)

    h = _norm_mod(x_ref[...], gain_ref[...], mod_ref[...], sh_idx, sc_idx)
    h_ref[...] = _pack_rows(h)
    h_hi, h_lo = _split_bf16(h)
    w_hi = wr_hi_ref[...]
    logits = jnp.dot(h_hi, w_hi, preferred_element_type=F32)
    logits += jnp.dot(h_lo, w_hi, preferred_element_type=F32)
    logits += jnp.dot(h_hi, wr_lo_ref[...], preferred_element_type=F32)
    g1, g2, e1, e2 = _route(logits + br_ref[...])

    tm = h.shape[0]
    lane = lax.broadcasted_iota(jnp.int32, (tm, LANES), 1)
    o1 = (lane == e1).astype(F32)
    o2 = (lane == e2).astype(F32)
    before = (lax.broadcasted_iota(jnp.int32, (tm, tm), 0) > lax.broadcasted_iota(jnp.int32, (tm, tm), 1)).astype(BF16)
    p1 = jnp.dot(before, o1.astype(BF16), preferred_element_type=F32)
    p2 = jnp.dot(before, o2.astype(BF16), preferred_element_type=F32)
    tot1 = jnp.sum(o1, axis=0, keepdims=True)
    tot2 = jnp.sum(o2, axis=0, keepdims=True)
    run = run_ref[0:1, :]
    rank1 = jnp.sum(o1 * (run + p1), axis=1, keepdims=True)
    rank2 = jnp.sum(o2 * (run + tot1 + p2), axis=1, keepdims=True)
    run = run + tot1 + tot2
    run_ref[0:1, :] = run
    cnt_ref[...] = jnp.broadcast_to(run, cnt_ref.shape)

    out = jnp.where(lane == 0, g1, 0.0)
    out = jnp.where(lane == 1, g2, out)
    out = jnp.where(lane == 2, e1.astype(F32), out)
    out = jnp.where(lane == 3, e2.astype(F32), out)
    out = jnp.where(lane == 4, rank1, out)
    out = jnp.where(lane == 5, rank2, out)
    r_ref[...] = out


def norm_mod(x, gains, layer, modseg, sh_idx, sc_idx, router=None):
    t, d = x.shape
    in_specs = [
        pl.BlockSpec((TM, d), lambda i: (i, 0)),
        pl.BlockSpec((None, 1, d), lambda i: (layer, 0, 0)),
        pl.BlockSpec((None, MOD_ROWS, d), lambda i: (i * TM // SEG, 0, 0)),
    ]
    h_spec = pl.BlockSpec((TM, d), lambda i: (i, 0))
    h_shape = jax.ShapeDtypeStruct((t, d), BF16)
    gains3 = gains.reshape(gains.shape[0], 1, d)
    if router is None:
        return pl.pallas_call(
            functools.partial(_norm_kernel, sh_idx=sh_idx, sc_idx=sc_idx),
            grid=(t // TM,), in_specs=in_specs, out_specs=h_spec, out_shape=h_shape,
            compiler_params=_cparams(1), name="norm_mod",
        )(x, gains3, modseg)
    wr_hi, wr_lo, br = router
    const = lambda i: (0, 0)
    return pl.pallas_call(
        functools.partial(_norm_route_kernel, sh_idx=sh_idx, sc_idx=sc_idx),
        grid=(t // TM,),
        in_specs=in_specs + [pl.BlockSpec((d, LANES), const), pl.BlockSpec((d, LANES), const),
                             pl.BlockSpec((1, LANES), const)],
        out_specs=[pl.BlockSpec((TM, d // 2), lambda i: (i, 0)), pl.BlockSpec((TM, LANES), lambda i: (i, 0)),
                   pl.BlockSpec((MOD_ROWS, LANES), const)],
        out_shape=[jax.ShapeDtypeStruct((t, d // 2), jnp.uint32), jax.ShapeDtypeStruct((t, LANES), F32),
                   jax.ShapeDtypeStruct((MOD_ROWS, LANES), F32)],
        scratch_shapes=[pltpu.VMEM((MOD_ROWS, LANES), F32)],
        compiler_params=_cparams(1), name="norm_route",
    )(x, gains3, modseg, wr_hi, wr_lo, br)


def _mm_kernel(h_ref, w_ref, o_ref, wbf_ref):
    @pl.when(pl.program_id(1) == 0)
    def _():
        wbf_ref[...] = w_ref[...].astype(BF16)

    o_ref[...] = jnp.dot(h_ref[...], wbf_ref[...], preferred_element_type=F32).astype(o_ref.dtype)


def _mm_resid_kernel(hp_ref, hs_ref, w_ref, x_ref, mod_ref, o_ref, wbf_ref, *, g_idx, n_prompt_tiles):
    i = pl.program_id(1)

    @pl.when(i == 0)
    def _():
        wbf_ref[...] = w_ref[...].astype(BF16)

    def finish(h):
        acc = jnp.dot(h, wbf_ref[...], preferred_element_type=F32)
        o_ref[...] = x_ref[...] + mod_ref[g_idx:g_idx + 1, :] * acc

    @pl.when(i < n_prompt_tiles)
    def _():
        finish(hp_ref[...])

    @pl.when(i >= n_prompt_tiles)
    def _():
        finish(hs_ref[...])


def matmul(h, w, layer, col0, n_cols, out_dtype):
    n_rows, k = h.shape
    tn = min(TN, n_cols)
    c0 = col0 // tn
    return pl.pallas_call(
        _mm_kernel,
        grid=(n_cols // tn, n_rows // TM),
        in_specs=[pl.BlockSpec((TM, k), lambda j, i: (i, 0)),
                  pl.BlockSpec((None, k, tn), lambda j, i: (layer, 0, c0 + j))],
        out_specs=pl.BlockSpec((TM, tn), lambda j, i: (i, j)),
        out_shape=jax.ShapeDtypeStruct((n_rows, n_cols), out_dtype),
        scratch_shapes=[pltpu.VMEM((k, tn), BF16)],
        compiler_params=_cparams(2), name="matmul",
    )(h, w)


def matmul_resid(hp, hs, w, layer, x, modseg, g_idx):
    k = hp.shape[1]
    t = hp.shape[0] + hs.shape[0]
    n = w.shape[-1]
    tn = min(TN, n)
    npt = hp.shape[0] // TM
    return pl.pallas_call(
        functools.partial(_mm_resid_kernel, g_idx=g_idx, n_prompt_tiles=npt),
        grid=(n // tn, t // TM),
        in_specs=[pl.BlockSpec((TM, k), lambda j, i: (jnp.minimum(i, npt - 1), 0)),
                  pl.BlockSpec((TM, k), lambda j, i: (jnp.maximum(i - npt, 0), 0)),
                  pl.BlockSpec((None, k, tn), lambda j, i: (layer, 0, j)),
                  pl.BlockSpec((TM, tn), lambda j, i: (i, j)),
                  pl.BlockSpec((None, MOD_ROWS, tn), lambda j, i: (i * TM // SEG, 0, j))],
        out_specs=pl.BlockSpec((TM, tn), lambda j, i: (i, j)),
        out_shape=jax.ShapeDtypeStruct((t, n), F32),
        scratch_shapes=[pltpu.VMEM((k, tn), BF16)],
        compiler_params=_cparams(2), name="matmul_resid",
    )(hp, hs, w, x, modseg)


def _rope(x, cos, sin):
    lane = lax.broadcasted_iota(jnp.int32, x.shape, 1)
    first_half = (lane % DA_QK_DIM) < (DA_QK_DIM // 2)
    partner = jnp.where(first_half, pltpu.roll(x, LANES - DA_QK_DIM // 2, 1), pltpu.roll(x, DA_QK_DIM // 2, 1))
    return x * cos + partner * sin


def _store_values(vb_ref, v):
    vb_ref[:, 0:LANES] = v.astype(BF16)
    vb_ref[:, LANES:2 * LANES] = jnp.ones((v.shape[0], LANES), BF16)


def _softmax_values(q_bf, kb, vb, kcb, vcb, bias=None):
    dn = (((1,), (1,)), ((), ()))
    s = lax.dot_general(q_bf, kb, dn, preferred_element_type=F32)
    if bias is not None:
        s = s + bias
    m = jnp.max(s, axis=1, keepdims=True)
    if kcb is not None:
        sc = lax.dot_general(q_bf, kcb, dn, preferred_element_type=F32)
        m = jnp.maximum(m, jnp.max(sc, axis=1, keepdims=True))
    acc = jnp.dot(jnp.exp((s - m).astype(BF16)), vb, preferred_element_type=F32)
    if kcb is not None:
        acc += jnp.dot(jnp.exp((sc - m).astype(BF16)), vcb, preferred_element_type=F32)
    return acc[:, 0:LANES], acc[:, LANES:2 * LANES]


def _softmax_parts(q_bf, kb):
    s = lax.dot_general(q_bf, kb, (((1,), (1,)), ((), ())), preferred_element_type=F32)
    p = jnp.exp(s - jnp.max(s, axis=1, keepdims=True))
    return p, jnp.sum(p, axis=1, keepdims=True)


PROMPT_HEADS_PER_STEP = 4


def _attn_kernel(*refs, diff, scale, out_scale):
    if diff:
        q_ref, k_ref, v_ref, lam_ref, g_ref, o_ref = refs
    else:
        q_ref, k_ref, v_ref, o_ref = refs
    for hh in range(PROMPT_HEADS_PER_STEP):
        lanes = slice(hh * HEAD_DIM, (hh + 1) * HEAD_DIM)
        q = q_ref[:, lanes].astype(F32) * scale
        kb = k_ref[:, lanes].astype(BF16)
        vb = v_ref[:, lanes].astype(BF16)
        if diff:
            lane = lax.broadcasted_iota(jnp.int32, q.shape, 1)
            p0, l0 = _softmax_parts(jnp.where(lane < DA_QK_DIM, q, 0.0).astype(BF16), kb)
            p1, l1 = _softmax_parts(jnp.where(lane >= DA_QK_DIM, q, 0.0).astype(BF16), kb)
            a = p0 * (1.0 / l0) - p1 * (lam_ref[0:1, 0:1] / l1)
            o = jnp.dot(a.astype(BF16), vb, preferred_element_type=F32)
            ms = jnp.mean(o * o, axis=-1, keepdims=True)
            o = (o * lax.rsqrt(ms + NORM_EPS) * g_ref[...]) * out_scale
        else:
            p, l = _softmax_parts(q.astype(BF16), kb)
            o = jnp.dot((p * (1.0 / l)).astype(BF16), vb, preferred_element_type=F32)
        o_ref[:, lanes] = o.astype(o_ref.dtype)


def attention(qkv, *, n_batch, n_seq, diff, lam=None, subln=None, out_scale=1.0, scale):
    width = PROMPT_HEADS_PER_STEP * HEAD_DIM
    groups = HEADS // PROMPT_HEADS_PER_STEP
    in_specs = [pl.BlockSpec((n_seq, width), lambda b, g: (b, g)),
                pl.BlockSpec((n_seq, width), lambda b, g: (b, groups + g)),
                pl.BlockSpec((n_seq, width), lambda b, g: (b, 2 * groups + g))]
    args = [qkv, qkv, qkv]
    if diff:
        in_specs += [pl.BlockSpec((1, LANES), lambda b, g: (0, 0))] * 2
        args += [lam, subln]
    return pl.pallas_call(
        functools.partial(_attn_kernel, diff=diff, scale=scale, out_scale=out_scale),
        grid=(n_batch, groups),
        in_specs=in_specs,
        out_specs=pl.BlockSpec((n_seq, width), lambda b, g: (b, g)),
        out_shape=jax.ShapeDtypeStruct((n_batch * n_seq, D_MODEL), BF16),
        compiler_params=_cparams(2), name="attention",
    )(*args)


LOG2_E = 1.4426950408889634


DA_KEY_CHUNK = 512
DA_SUM_ROWS = 16


def _diff_long_kernel(q_ref, k_ref, v_ref, ck_ref, cv_ref, cosq_ref, sinq_ref, cosk_ref, sink_ref, lam_ref, g_ref,
                      o_ref, kb_ref, vt_ref, s_ref, p_ref, *, n_lat, n_ctx, scale, out_scale):
    n_k = n_lat + n_ctx
    n_chunks = n_k // DA_KEY_CHUNK
    dv = v_ref.shape[1]

    @pl.when(pl.program_id(2) == 0)
    def _():
        kb_ref[0:n_lat, :] = _rope(k_ref[...].astype(F32), cosk_ref[...], sink_ref[...]).astype(BF16)
        kb_ref[n_lat:n_k, :] = ck_ref[...].astype(BF16)
        vt_ref[0:dv, 0:n_lat] = v_ref[...].astype(F32).T.astype(BF16)
        vt_ref[0:dv, n_lat:n_k] = cv_ref[...].astype(F32).T.astype(BF16)
        vt_ref[dv:dv + DA_SUM_ROWS, :] = jnp.ones((DA_SUM_ROWS, n_k), BF16)

    q = _rope(q_ref[...].astype(F32), cosq_ref[...], sinq_ref[...]) * (scale * LOG2_E)
    lane = lax.broadcasted_iota(jnp.int32, q.shape, 1)
    qs = (jnp.where(lane < DA_QK_DIM, q, 0.0).astype(BF16), jnp.where(lane >= DA_QK_DIM, q, 0.0).astype(BF16))
    dn = (((1,), (1,)), ((), ()))
    chunks = [pl.ds(j * DA_KEY_CHUNK, DA_KEY_CHUNK) for j in range(n_chunks)]

    def scores(c, rows, m):
        st = lax.dot_general(kb_ref[rows, :], qs[c], dn, preferred_element_type=F32)
        s_ref[c, rows, :] = st
        cm = jnp.max(st, axis=0, keepdims=True)
        return cm if m is None else jnp.maximum(m, cm)

    def exps(c, rows, m):
        p_ref[c, rows, :] = jnp.exp2((s_ref[c, rows, :] - m).astype(BF16))

    def values(c, rows, acc):
        pv = jnp.dot(vt_ref[:, rows], p_ref[c, rows, :], preferred_element_type=F32)
        return pv if acc is None else acc + pv

    m0 = m1 = acc0 = acc1 = None
    for rows in chunks:
        m0 = scores(0, rows, m0)
    for rows in chunks:
        m1 = scores(1, rows, m1)
        exps(0, rows, m0)
    for rows in chunks:
        exps(1, rows, m1)
        acc0 = values(0, rows, acc0)
    for rows in chunks:
        acc1 = values(1, rows, acc1)
    o_t = acc0[0:dv, :] / acc0[dv:dv + 1, :] - lam_ref[0:1, 0:1] * (acc1[0:dv, :] / acc1[dv:dv + 1, :])
    o = o_t.T
    ms = jnp.mean(o * o, axis=-1, keepdims=True)
    o_ref[...] = ((o * lax.rsqrt(ms + NORM_EPS) * g_ref[...]) * out_scale).astype(o_ref.dtype)


def diff_attention_long(qkv, *, row0, n_batch, n_seq, tq, ctx, rope_tabs, lam, subln, out_scale, scale):
    nq = n_seq // tq
    qb0, kb0 = row0 // tq, row0 // n_seq
    ck, cv, cl = ctx
    n_ctx = ck.shape[2]
    cos, sin = rope_tabs
    cspec = pl.BlockSpec((None, None, n_ctx, LANES), lambda b, h, i: (b, cl, 0, h))
    qtab = pl.BlockSpec((tq, LANES), lambda b, h, i: (i, 0))
    ktab = pl.BlockSpec((n_seq, LANES), lambda b, h, i: (0, 0))
    row = pl.BlockSpec((1, LANES), lambda b, h, i: (0, 0))
    return pl.pallas_call(
        functools.partial(_diff_long_kernel, n_lat=n_seq, n_ctx=n_ctx, scale=scale, out_scale=out_scale),
        grid=(n_batch, HEADS, nq),
        in_specs=[pl.BlockSpec((tq, LANES), lambda b, h, i: (qb0 + b * nq + i, h)),
                  pl.BlockSpec((n_seq, LANES), lambda b, h, i: (kb0 + b, HEADS + h)),
                  pl.BlockSpec((n_seq, LANES), lambda b, h, i: (kb0 + b, 2 * HEADS + h)),
                  cspec, cspec, qtab, qtab, ktab, ktab, row, row],
        out_specs=pl.BlockSpec((tq, LANES), lambda b, h, i: (b * nq + i, h)),
        out_shape=jax.ShapeDtypeStruct((n_batch * n_seq, D_MODEL), BF16),
        scratch_shapes=[pltpu.VMEM((n_seq + n_ctx, LANES), BF16),
                        pltpu.VMEM((LANES + DA_SUM_ROWS, n_seq + n_ctx), BF16),
                        pltpu.VMEM((2, n_seq + n_ctx, tq), F32), pltpu.VMEM((2, n_seq + n_ctx, tq), BF16)],
        compiler_params=_cparams(3), name="diff_attention_long",
    )(qkv, qkv, qkv, ck, cv, cos, sin, cos, sin, lam, subln)


def _na_key_row0(t):
    return jnp.clip(t * NA_TILE_ROWS - NA_WIN_ROWS // 2, 0, DEC_SEQ // GRID_W - NA_KEY_ROWS)


def _na_kernel(q_ref, k_ref, v_ref, ck_ref, cv_ref, bias_ref, o_ref, kb_ref, vb_ref, kcb_ref, vcb_ref, *, scale):
    t = pl.program_id(2)

    @pl.when(t == 0)
    def _():
        kb_ref[...] = k_ref[...].astype(BF16)
        _store_values(vb_ref, v_ref[...])
        kcb_ref[...] = ck_ref[...].astype(BF16)
        _store_values(vcb_ref, cv_ref[...])

    n_keys = NA_KEY_ROWS * GRID_W
    k0 = pl.multiple_of(_na_key_row0(t) * GRID_W, GRID_W)
    kw = kb_ref[pl.ds(k0, n_keys), :]
    vw = vb_ref[pl.ds(k0, n_keys), :]
    q = (q_ref[...].astype(F32) * scale).astype(BF16)
    n0, l0 = _softmax_values(q, kw, vw, kcb_ref[...], vcb_ref[...], bias=bias_ref[...])
    o_ref[...] = (n0 / l0).astype(o_ref.dtype)


def na_bias_table(rpb):
    rows = DEC_SEQ // GRID_W
    n_h = rpb.shape[0]
    col = jnp.arange(GRID_W)
    col0 = jnp.clip(col - NA_WIN_COLS // 2, 0, GRID_W - NA_WIN_COLS)
    col_ok = (col[None, :] >= col0[:, None]) & (col[None, :] < col0[:, None] + NA_WIN_COLS)
    dc_idx = jnp.clip(col[None, :] - col[:, None], -(NA_WIN_COLS - 1), NA_WIN_COLS - 1) + NA_WIN_COLS - 1
    pick = (dc_idx[None] == jnp.arange(2 * NA_WIN_COLS - 1)[:, None, None]).astype(F32)
    band = jnp.einsum('hrc,cqk->hrqk', rpb.astype(F32), pick, precision=lax.Precision.HIGHEST)
    band = jnp.where(col_ok, band, -jnp.inf)
    masked = jnp.full((n_h, GRID_W, GRID_W), -jnp.inf, F32)
    classes = []
    for tile in (0, 1, rows // NA_TILE_ROWS - 1):
        u0 = min(max(tile * NA_TILE_ROWS - NA_WIN_ROWS // 2, 0), rows - NA_KEY_ROWS)
        q_rows = []
        for jq in range(NA_TILE_ROWS):
            r = tile * NA_TILE_ROWS + jq
            r0 = min(max(r - NA_WIN_ROWS // 2, 0), rows - NA_WIN_ROWS)
            blocks = [band[:, u0 + u - r + NA_WIN_ROWS - 1] if r0 <= u0 + u < r0 + NA_WIN_ROWS else masked
                      for u in range(NA_KEY_ROWS)]
            q_rows.append(jnp.concatenate(blocks, axis=-1))
        classes.append(jnp.concatenate(q_rows, axis=1))
    return jnp.stack(classes, axis=1)


def na_attention(qkv, row0, cache_k, cache_v, cl, bias, scale):
    tq = NA_TILE_ROWS * GRID_W
    nt = DEC_SEQ // tq
    qb0, kb0 = row0 // tq, row0 // DEC_SEQ
    n_ctx = cache_k.shape[2]
    cspec = pl.BlockSpec((None, None, n_ctx, LANES), lambda b, h, t: (b, cl, 0, h))
    scratch = [pltpu.VMEM((DEC_SEQ, LANES), BF16), pltpu.VMEM((DEC_SEQ, 2 * LANES), BF16),
               pltpu.VMEM((n_ctx, LANES), BF16), pltpu.VMEM((n_ctx, 2 * LANES), BF16)]
    return pl.pallas_call(
        functools.partial(_na_kernel, scale=scale),
        grid=(DEC_BATCH, HEADS, nt),
        in_specs=[
            pl.BlockSpec((tq, LANES), lambda b, h, t: (qb0 + b * nt + t, h)),
            pl.BlockSpec((DEC_SEQ, LANES), lambda b, h, t: (kb0 + b, HEADS + h)),
            pl.BlockSpec((DEC_SEQ, LANES), lambda b, h, t: (kb0 + b, 2 * HEADS + h)),
            cspec, cspec,
            pl.BlockSpec((None, None, tq, NA_KEY_ROWS * GRID_W),
                         lambda b, h, t: (h, jnp.minimum(t, 1) + t // (nt - 1), 0, 0)),
        ],
        out_specs=pl.BlockSpec((tq, LANES), lambda b, h, t: (b * nt + t, h)),
        out_shape=jax.ShapeDtypeStruct((T_SAMPLE, D_MODEL), BF16),
        scratch_shapes=scratch,
        compiler_params=_cparams(3), name="na_attention",
    )(qkv, qkv, qkv, cache_k, cache_v, bias)


HG_LEVELS = HG_TILE.bit_length() - 1


def hgrn_masks():
    t = jnp.arange(HG_TILE)
    pair, qsel = [], []
    for d in range(2):
        for lvl in range(HG_LEVELS):
            upper = ((t >> lvl) & 1) == 1
            is_q = jnp.logical_not(upper) if d else upper
            same = (t[:, None] >> (lvl + 1)) == (t[None, :] >> (lvl + 1))
            pair.append(same & is_q[:, None] & jnp.logical_not(is_q)[None, :])
            qsel.append(jnp.broadcast_to(is_q[:, None], (HG_TILE, HG_TILE)))
    shape = (2, HG_LEVELS, HG_TILE, HG_TILE)
    return jnp.stack(pair).astype(F32).reshape(shape), jnp.stack(qsel).astype(F32).reshape(shape)


def _hgrn_tile(q, kk, f, v_f32, st_bf, d, pair_ref, qsel_ref):
    c = q.shape[0]
    rev = d == 1
    dn = (((1,), (1,)), ((), ()))
    px = f
    py = jnp.ones_like(f)
    pt = f
    operands = []
    for lvl in range(HG_LEVELS):
        m = 1 << lvl
        operands.append(((q * px).astype(BF16), (kk * py).astype(BF16)))
        if m % 8 == 0:
            halves = lambda a: (a.reshape(c // (2 * m), 2, m, a.shape[1])[:, 0],
                                a.reshape(c // (2 * m), 2, m, a.shape[1])[:, 1])
            join = lambda lo, hi: jnp.stack([lo, hi], axis=1).reshape(c, lo.shape[-1])
            (x_lo, x_hi), (y_lo, y_hi), (t_lo, t_hi) = halves(px), halves(py), halves(pt)
            if rev:
                px, py = join(x_lo * t_hi, x_hi), join(y_lo, y_hi * t_lo)
            else:
                px, py = join(x_lo, x_hi * t_lo), join(y_lo * t_hi, y_hi)
            both = t_lo * t_hi
            pt = join(both, both)
        else:
            below = pltpu.roll(pt, m, 0)
            above = pltpu.roll(pt, c - m, 0)
            is_q = qsel_ref[d, lvl] > 0.5
            sib = jnp.where(is_q, above, below) if rev else jnp.where(is_q, below, above)
            grown = jnp.where(is_q, px, py) * sib
            px = jnp.where(is_q, grown, px)
            py = jnp.where(is_q, py, grown)
            pt = pt * sib
    prods = [lax.dot_general(qp, kp, dn, preferred_element_type=F32) for qp, kp in operands]
    att = prods[0] * pair_ref[d, 0]
    for lvl in range(1, HG_LEVELS):
        att = att + prods[lvl] * pair_ref[d, lvl]
    tot = pt[0:1, :]
    v_bf = v_f32.astype(BF16)
    diag = jnp.sum(q * kk, axis=1, keepdims=True)
    o = jnp.dot(att.astype(BF16), v_bf, preferred_element_type=F32) + diag * v_f32
    o += lax.dot_general((q * px).astype(BF16), st_bf, dn, preferred_element_type=F32)
    inc = jnp.dot(v_f32.T.astype(BF16), (kk * py).astype(BF16), preferred_element_type=F32)
    return o, inc, tot


def _hgrn_kernel(*refs, n, has_s0, emit_state, hps):
    it = iter(refs)
    q_ref, i_ref, zf_ref, zb_ref, g_ref = next(it), next(it), next(it), next(it), next(it)
    s0_ref = next(it) if has_s0 else None
    lb_ref, gn_ref, pair_ref, qsel_ref = next(it), next(it), next(it), next(it)
    o_ref = next(it)
    sfin_ref = next(it) if emit_state else None
    of_ref, ob_ref, st_ref = next(it), next(it), next(it)
    nt = n // HG_TILE
    heads = [slice(hh * HEAD_DIM, (hh + 1) * HEAD_DIM) for hh in range(hps)]

    def gates(z, lanes):
        lb, one_m_lb = lb_ref[0:1, lanes], lb_ref[1:2, lanes]
        e = jnp.exp(-jnp.abs(z))
        r = 1.0 / (1.0 + e)
        pos = z > 0
        f = lb + one_m_lb * (jnp.where(pos, 1.0, e) * r)
        kk = one_m_lb * (jnp.where(pos, e, 1.0) * r)
        return f, kk

    for d in range(2):
        for hh in range(hps):
            if has_s0:
                st_ref[d, hh] = s0_ref[d, hh].T
            else:
                st_ref[d, hh] = jnp.zeros((HEAD_DIM, HEAD_DIM), F32)

    def scan_step(step, carry):
        for hh, lanes in enumerate(heads):
            for d, (z_ref, out_ref) in enumerate(((zf_ref, of_ref), (zb_ref, ob_ref))):
                ti = (nt - 1 - step) if d else step
                rows = pl.ds(pl.multiple_of(ti * HG_TILE, HG_TILE), HG_TILE)
                f, kk = gates(z_ref[rows, lanes], lanes)
                qr = q_ref[rows, lanes].astype(F32)
                q = (qr / (1.0 + jnp.exp(-qr))) * (HEAD_DIM ** -0.5)
                st = st_ref[d, hh]
                o, inc, tot = _hgrn_tile(q, kk, f, i_ref[rows, lanes].astype(F32), st.astype(BF16), d,
                                         pair_ref, qsel_ref)
                st_ref[d, hh] = st * tot + inc
                out_ref[rows, lanes] = o
        return carry

    lax.fori_loop(0, nt, scan_step, 0)

    def finish(ti, carry):
        rows = pl.ds(pl.multiple_of(ti * HG_TILE, HG_TILE), HG_TILE)
        for lanes in heads:
            o = of_ref[rows, lanes] + ob_ref[rows, lanes]
            ms = jnp.mean(o * o, axis=-1, keepdims=True)
            o = o * lax.rsqrt(ms + NORM_EPS) * gn_ref[...]
            g = g_ref[rows, lanes].astype(F32)
            o_ref[rows, lanes] = (o * (g / (1.0 + jnp.exp(-g)))).astype(o_ref.dtype)
        return carry

    lax.fori_loop(0, nt, finish, 0)
    if emit_state:
        for d in range(2):
            for hh in range(hps):
                sfin_ref[d, hh] = st_ref[d, hh].T


def hgrn_scan(qi, z, g, lbp, layer, gnorm, masks, *, row0, n_batch, n_seq, hps, s0=None, s0_layer=0,
              emit_state=False):
    rb0 = row0 // n_seq
    width = hps * HEAD_DIM
    groups = HEADS // hps
    blk = lambda c0: pl.BlockSpec((n_seq, width), lambda b, h: (rb0 + b, c0 + h))
    in_specs = [blk(0), blk(groups), blk(0), blk(groups), blk(0)]
    args = [qi, qi, z, z, g]
    if s0 is not None:
        in_specs.append(pl.BlockSpec((None, None, 2, hps, HEAD_DIM, HEAD_DIM),
                                     lambda b, h: (b, s0_layer, 0, h, 0, 0)))
        args.append(s0)
    mask_spec = pl.BlockSpec((2, HG_LEVELS, HG_TILE, HG_TILE), lambda b, h: (0, 0, 0, 0))
    in_specs += [pl.BlockSpec((None, MOD_ROWS, width), lambda b, h: (layer, 0, h)),
                 pl.BlockSpec((1, LANES), lambda b, h: (0, 0)), mask_spec, mask_spec]
    args += [lbp, gnorm, masks[0], masks[1]]
    out_specs = [pl.BlockSpec((n_seq, width), lambda b, h: (b, h))]
    out_shape = [jax.ShapeDtypeStruct((n_batch * n_seq, D_MODEL), BF16)]
    if emit_state:
        out_specs.append(pl.BlockSpec((None, 2, hps, HEAD_DIM, HEAD_DIM), lambda b, h: (b, 0, h, 0, 0)))
        out_shape.append(jax.ShapeDtypeStruct((n_batch, 2, HEADS, HEAD_DIM, HEAD_DIM), F32))
    res = pl.pallas_call(
        functools.partial(_hgrn_kernel, n=n_seq, has_s0=s0 is not None, emit_state=emit_state, hps=hps),
        grid=(n_batch, groups),
        in_specs=in_specs, out_specs=out_specs, out_shape=out_shape,
        scratch_shapes=[pltpu.VMEM((n_seq, width), F32), pltpu.VMEM((n_seq, width), F32),
                        pltpu.VMEM((2, hps, HEAD_DIM, HEAD_DIM), F32)],
        compiler_params=_cparams(2), name="hgrn_scan",
    )(*args)
    return res if emit_state else res[0]


def _expert_kernel(be_ref, first_ref, nused_ref, x_ref, wg_ref, wu_ref, wd_ref, o_ref, wg_bf, wu_bf, wd_bf):
    g = pl.program_id(0)

    @pl.when(g < nused_ref[0])
    def _():
        @pl.when(first_ref[g] == 1)
        def _():
            wg_bf[...] = wg_ref[...].astype(BF16)
            wu_bf[...] = wu_ref[...].astype(BF16)
            wd_bf[...] = wd_ref[...].astype(BF16)

        x = _unpack_rows(x_ref[...]).astype(BF16)
        a = jnp.dot(x, wg_bf[...], preferred_element_type=F32)
        u = jnp.dot(x, wu_bf[...], preferred_element_type=F32)
        hmid = (a / (1.0 + jnp.exp(-a))) * u
        o_ref[...] = _pack_rows(jnp.dot(hmid.astype(BF16), wd_bf[...], preferred_element_type=F32))

    @pl.when(g >= nused_ref[0])
    def _():
        o_ref[...] = jnp.zeros_like(o_ref)


def experts(xb, block_e, first, n_used, w_gate, w_up, w_down, layer):
    cap, dp = xb.shape
    d = 2 * dp
    nb = cap // MOE_ROWS
    de = w_gate.shape[-1]
    grid_spec = pltpu.PrefetchScalarGridSpec(
        num_scalar_prefetch=3,
        grid=(nb,),
        in_specs=[
            pl.BlockSpec((MOE_ROWS, dp), lambda g, be, fi, nu: (g, 0)),
            pl.BlockSpec((None, None, d, de), lambda g, be, fi, nu: (layer, be[g], 0, 0)),
            pl.BlockSpec((None, None, d, de), lambda g, be, fi, nu: (layer, be[g], 0, 0)),
            pl.BlockSpec((None, None, de, d), lambda g, be, fi, nu: (layer, be[g], 0, 0)),
        ],
        out_specs=pl.BlockSpec((MOE_ROWS, dp), lambda g, be, fi, nu: (g, 0)),
        scratch_shapes=[pltpu.VMEM((d, de), BF16), pltpu.VMEM((d, de), BF16), pltpu.VMEM((de, d), BF16)],
    )
    return pl.pallas_call(
        _expert_kernel, grid_spec=grid_spec,
        out_shape=jax.ShapeDtypeStruct((cap, dp), jnp.uint32),
        compiler_params=_cparams(1), name="experts",
    )(block_e, first, n_used, xb, w_gate, w_up, w_down)


def _dispatch_kernel(pos_ref, segend_ref, cnt_ref, nused_ref, h_ref, xb_hbm, zero_ref, sem, zsem):
    i = pl.program_id(0)
    rows = h_ref.shape[0]
    n_blocks = xb_hbm.shape[0] // MOE_ROWS

    def zero_block(start):
        return pltpu.make_async_copy(zero_ref, xb_hbm.at[pl.ds(pl.multiple_of(start, MOE_ROWS), MOE_ROWS), :], zsem)

    @pl.when(i == 0)
    def _():
        zero_ref[...] = jnp.zeros_like(zero_ref)

        def start(e, c):
            @pl.when(cnt_ref[e] > 0)
            def _():
                zero_block(segend_ref[e] - MOE_ROWS).start()
            return c

        def wait(e, c):
            @pl.when(cnt_ref[e] > 0)
            def _():
                zero_block(segend_ref[e] - MOE_ROWS).wait()
            return c

        def start_tail(g, c):
            zero_block(g * MOE_ROWS).start()
            return c

        def wait_tail(g, c):
            zero_block(g * MOE_ROWS).wait()
            return c

        lax.fori_loop(0, MOE_EXPERTS, start, 0)
        lax.fori_loop(nused_ref[0], n_blocks, start_tail, 0)
        lax.fori_loop(0, MOE_EXPERTS, wait, 0)
        lax.fori_loop(nused_ref[0], n_blocks, wait_tail, 0)

    def issue(r, c):
        src = h_ref.at[pl.ds(r, 1), :]
        pltpu.make_async_copy(src, xb_hbm.at[pl.ds(pos_ref[2 * r], 1), :], sem).start(priority=0)
        pltpu.make_async_copy(src, xb_hbm.at[pl.ds(pos_ref[2 * r + 1], 1), :], sem).start(priority=1)
        return c

    lax.fori_loop(0, rows, issue, 0, unroll=8)
    pltpu.make_async_copy(h_ref, xb_hbm.at[pl.ds(0, rows), :], sem).wait()
    pltpu.make_async_copy(h_ref, xb_hbm.at[pl.ds(0, rows), :], sem).wait()


def dispatch(h, pos_flat, seg_end, counts, n_used, cap):
    t, d = h.shape
    rows = 512
    grid_spec = pltpu.PrefetchScalarGridSpec(
        num_scalar_prefetch=0,
        grid=(t // rows,),
        in_specs=[pl.BlockSpec((2 * rows,), lambda i: (i,), memory_space=pltpu.SMEM),
                  pl.BlockSpec(memory_space=pltpu.SMEM),
                  pl.BlockSpec(memory_space=pltpu.SMEM),
                  pl.BlockSpec(memory_space=pltpu.SMEM),
                  pl.BlockSpec((rows, d), lambda i: (i, 0))],
        out_specs=pl.BlockSpec(memory_space=pl.ANY),
        scratch_shapes=[pltpu.VMEM((MOE_ROWS, d), h.dtype), pltpu.SemaphoreType.DMA(()),
                        pltpu.SemaphoreType.DMA(())],
    )
    return pl.pallas_call(
        _dispatch_kernel, grid_spec=grid_spec,
        out_shape=jax.ShapeDtypeStruct((cap, d), h.dtype),
        compiler_params=_cparams(1), name="dispatch",
    )(pos_flat, seg_end, counts, n_used, h)


def _combine_kernel(pos_ref, posn_ref, x_ref, r_ref, mod_ref, gain_ref, modn_ref, yb_hbm, *rest, g_idx, final):
    if final:
        y_ref, buf, sem = rest
    else:
        o_ref, h_ref, buf, sem = rest
    tm = x_ref.shape[0]
    i = pl.program_id(0)
    slot = i % 2

    def gather(p_ref, s):
        def issue(r, c):
            pltpu.make_async_copy(yb_hbm.at[pl.ds(p_ref[2 * r], 1), :], buf.at[s, 0, pl.ds(r, 1), :],
                                  sem.at[s]).start(priority=0)
            pltpu.make_async_copy(yb_hbm.at[pl.ds(p_ref[2 * r + 1], 1), :], buf.at[s, 1, pl.ds(r, 1), :],
                                  sem.at[s]).start(priority=1)
            return c

        lax.fori_loop(0, tm, issue, 0, unroll=8)

    @pl.when(i == 0)
    def _():
        gather(pos_ref, slot)

    @pl.when(i + 1 < pl.num_programs(0))
    def _():
        gather(posn_ref, 1 - slot)

    pltpu.make_async_copy(yb_hbm.at[pl.ds(0, tm), :], buf.at[slot, 0], sem.at[slot]).wait()
    pltpu.make_async_copy(yb_hbm.at[pl.ds(0, tm), :], buf.at[slot, 1], sem.at[slot]).wait()
    r = r_ref[...]
    y = r[:, 0:1] * _unpack_rows(buf[slot, 0]) + r[:, 1:2] * _unpack_rows(buf[slot, 1])
    x = x_ref[...] + mod_ref[g_idx:g_idx + 1, :] * y
    if final:
        ms = jnp.mean(x * x, axis=-1, keepdims=True)
        y_ref[...] = x * lax.rsqrt(ms + NORM_EPS) * gain_ref[...]
    else:
        o_ref[...] = x
        h_ref[...] = _norm_mod(x, gain_ref[...], modn_ref[...], 0, 1).astype(BF16)


def combine(x, yb, pos_flat, route, modseg, g_idx, gains, gain_idx, modseg_next, final):
    t, d = x.shape
    tm = 256
    row_spec = pl.BlockSpec((tm, d), lambda i: (i, 0))
    mod_spec = pl.BlockSpec((None, MOD_ROWS, d), lambda i: (i * tm // SEG, 0, 0))
    if final:
        out_specs, out_shape = row_spec, jax.ShapeDtypeStruct((t, d), F32)
    else:
        out_specs = [row_spec, row_spec]
        out_shape = [jax.ShapeDtypeStruct((t, d), F32), jax.ShapeDtypeStruct((t, d), BF16)]
    grid_spec = pltpu.PrefetchScalarGridSpec(
        num_scalar_prefetch=0,
        grid=(t // tm,),
        in_specs=[pl.BlockSpec((2 * tm,), lambda i: (i,), memory_space=pltpu.SMEM),
                  pl.BlockSpec((2 * tm,), lambda i: (jnp.minimum(i + 1, t // tm - 1),), memory_space=pltpu.SMEM),
                  row_spec,
                  pl.BlockSpec((tm, LANES), lambda i: (i, 0)),
                  mod_spec,
                  pl.BlockSpec((None, 1, d), lambda i: (gain_idx, 0, 0)),
                  mod_spec,
                  pl.BlockSpec(memory_space=pl.ANY)],
        out_specs=out_specs,
        scratch_shapes=[pltpu.VMEM((2, 2, tm, d // 2), jnp.uint32), pltpu.SemaphoreType.DMA((2,))],
    )
    return pl.pallas_call(
        functools.partial(_combine_kernel, g_idx=g_idx, final=final), grid_spec=grid_spec, out_shape=out_shape,
        compiler_params=_cparams(1), name="combine",
    )(pos_flat, pos_flat, x, route, modseg, gains.reshape(gains.shape[0], 1, d), modseg_next, yb)


def moe_plan(route, cnt):
    t = route.shape[0]
    counts = cnt[0, :MOE_EXPERTS].astype(jnp.int32)
    padded = (counts + MOE_ROWS - 1) // MOE_ROWS * MOE_ROWS
    seg_end = jnp.cumsum(padded).astype(jnp.int32)
    seg_start = seg_end - padded
    e = route[:, 2:4].astype(jnp.int32)
    rank = route[:, 4:6].astype(jnp.int32)
    onehot = e[:, :, None] == jnp.arange(MOE_EXPERTS, dtype=jnp.int32)
    pos = rank + jnp.sum(jnp.where(onehot, seg_start, 0), axis=-1)
    n_blocks = -(-(2 * t + MOE_EXPERTS * (MOE_ROWS - 1)) // MOE_ROWS)
    blk0 = jnp.arange(n_blocks, dtype=jnp.int32) * MOE_ROWS
    block_e = jnp.minimum(jnp.sum(seg_end[None, :] <= blk0[:, None], axis=1), MOE_EXPERTS - 1).astype(jnp.int32)
    first = jnp.concatenate([jnp.ones((1,), jnp.int32), (block_e[1:] != block_e[:-1]).astype(jnp.int32)])
    n_used = (seg_end[-1] // MOE_ROWS).astype(jnp.int32).reshape(1)
    return pos.reshape(-1).astype(jnp.int32), seg_end, counts, block_e, first, n_used, n_blocks * MOE_ROWS


def _rope_tables(n, dim):
    quarter = dim // 4
    inv = jnp.power(ROPE_BASE, -jnp.arange(quarter, dtype=F32) / quarter)
    t = jnp.arange(n)
    row = (t // GRID_W).astype(F32)
    col = (t % GRID_W).astype(F32)
    ang = jnp.concatenate([row[:, None] * inv, col[:, None] * inv], axis=-1)
    cos, sin = jnp.cos(ang), jnp.sin(ang)
    reps = LANES // dim
    cos_l = jnp.tile(jnp.concatenate([cos, cos], axis=-1), (1, reps))
    sin_l = jnp.tile(jnp.concatenate([-sin, sin], axis=-1), (1, reps))
    return cos_l, sin_l


def kernel(x_prompt, x_sample, state_hgrn, cache_diff_k, cache_diff_v, cache_na_k, cache_na_v, c, c_ctx, norm_mix, norm_ffn, w_ada, b_ada, router_group_w, router_group_b, router_expert_w, router_expert_b, expert_w_gate, expert_w_up, expert_w_down, hg_w_in, hg_lb_logits, hg_gnorm, hg_w_out, da_w_qkv, da_lambda, da_subln, da_w_out, na_w_qkv, na_rpb, na_w_out, final_norm):
    d = D_MODEL
    x = jnp.concatenate([x_prompt.reshape(T_PROMPT, d), x_sample.reshape(T_SAMPLE, d)], axis=0)

    cond8 = jnp.zeros((MOD_ROWS, d), F32).at[0].set(c_ctx).at[1:1 + DEC_BATCH].set(c)
    mod_all = ada_all(cond8, w_ada, b_ada)
    seg_cond = jnp.array([0] * (T_PROMPT // SEG) + list(range(1, 1 + DEC_BATCH)), jnp.int32)
    mod_all = mod_all.reshape(DEPTH, MOD_ROWS, N_MOD, d)[:, seg_cond]
    mod_all = jnp.pad(mod_all, ((0, 0), (0, 0), (0, MOD_ROWS - N_MOD), (0, 0)))

    p = jax.nn.softmax(hg_lb_logits.astype(F32), axis=0)
    cum = jnp.cumsum(p, axis=0)
    lb = cum - cum[0:1]
    lbp = jnp.stack([lb, 1.0 - lb], axis=1)
    lbp = jnp.pad(lbp, ((0, 0), (0, MOD_ROWS - 2), (0, 0)))

    wr = jnp.concatenate([router_group_w, router_expert_w], axis=-1)
    wr = jnp.pad(wr, ((0, 0), (0, 0), (0, LANES - wr.shape[-1])))
    wr_hi = wr.astype(BF16)
    wr_lo = (wr - wr_hi.astype(F32)).astype(BF16)
    br = jnp.concatenate([router_group_b, router_expert_b], axis=-1).astype(F32)
    br = jnp.pad(br, ((0, 0), (0, LANES - br.shape[-1])))[:, None, :]

    hg_masks = hgrn_masks()
    cos_l, sin_l = _rope_tables(DEC_SEQ, DA_QK_DIM)
    cdk = cache_diff_k.reshape(DEC_BATCH, -1, PAST_LEN, d)
    cdv = cache_diff_v.reshape(DEC_BATCH, -1, PAST_LEN, d)
    cnk = cache_na_k.reshape(DEC_BATCH, -1, PAST_LEN, d)
    cnv = cache_na_v.reshape(DEC_BATCH, -1, PAST_LEN, d)

    new_hg, new_dk, new_dv, new_nk, new_nv = [], [], [], [], []
    h = norm_mod(x, norm_mix, 0, mod_all[0], 0, 1)
    for i in range(DEPTH):
        kind, j = i % N_MIXERS, i // N_MIXERS
        modseg = mod_all[i]
        if kind == 0:
            qi = matmul(h, hg_w_in, j, 0, 2 * d, BF16)
            z = matmul(h, hg_w_in, j, 2 * d, 2 * d, F32)
            g = matmul(h, hg_w_in, j, 4 * d, d, BF16)
            gn = hg_gnorm[j].reshape(1, HEAD_DIM)
            op, s_fin = hgrn_scan(qi, z, g, lbp, i, gn, hg_masks, row0=0, n_batch=BATCH, n_seq=SEQ,
                                  hps=HG_HEADS_SHORT, emit_state=True)
            os_ = hgrn_scan(qi, z, g, lbp, i, gn, hg_masks, row0=T_PROMPT, n_batch=DEC_BATCH, n_seq=DEC_SEQ,
                            hps=HG_HEADS_LONG,
                            s0=state_hgrn, s0_layer=j)
            new_hg.append(s_fin)
            w_out = hg_w_out
        elif kind == 1:
            lam_init = 0.8 - 0.6 * math.exp(-0.3 * i)
            lp = da_lambda[j].astype(F32)
            lam = jnp.exp(jnp.sum(lp[0] * lp[1])) - jnp.exp(jnp.sum(lp[2] * lp[3])) + lam_init
            lam_row = jnp.full((1, LANES), lam, F32)
            subln = da_subln[j].reshape(1, HEAD_DIM).astype(F32)
            qkv = matmul(h, da_w_qkv, j, 0, 3 * d, F32)
            scale = DA_QK_DIM ** -0.5
            op = attention(qkv, n_batch=BATCH, n_seq=SEQ, diff=True, lam=lam_row, subln=subln,
                           out_scale=1.0 - lam_init, scale=scale)
            os_ = diff_attention_long(qkv, row0=T_PROMPT, n_batch=DEC_BATCH, n_seq=DEC_SEQ, tq=512,
                                      ctx=(cdk, cdv, j), rope_tabs=(cos_l, sin_l), lam=lam_row, subln=subln,
                                      out_scale=1.0 - lam_init, scale=scale)
            new_dk.append(qkv[:T_PROMPT, d:2 * d].reshape(BATCH, SEQ, HEADS, 2, DA_QK_DIM))
            new_dv.append(qkv[:T_PROMPT, 2 * d:].reshape(BATCH, SEQ, HEADS, HEAD_DIM))
            w_out = da_w_out
        else:
            qkv = matmul(h, na_w_qkv, j, 0, 3 * d, F32)
            scale = HEAD_DIM ** -0.5
            op = attention(qkv, n_batch=BATCH, n_seq=SEQ, diff=False, scale=scale)
            os_ = na_attention(qkv, T_PROMPT, cnk, cnv, j, na_bias_table(na_rpb[j]), scale)
            new_nk.append(qkv[:T_PROMPT, d:2 * d].reshape(BATCH, SEQ, HEADS, HEAD_DIM))
            new_nv.append(qkv[:T_PROMPT, 2 * d:].reshape(BATCH, SEQ, HEADS, HEAD_DIM))
            w_out = na_w_out
        x = matmul_resid(op, os_, w_out, j, x, modseg, 2)

        hf, route, cnt = norm_mod(x, norm_ffn, i, modseg, 3, 4, router=(wr_hi[i], wr_lo[i], br[i]))
        pos, seg_end, counts, block_e, first, n_used, cap = moe_plan(route, cnt)
        xb = dispatch(hf, pos, seg_end, counts, n_used, cap)
        yb = experts(xb, block_e, first, n_used, expert_w_gate, expert_w_up, expert_w_down, i)
        if i + 1 < DEPTH:
            x, h = combine(x, yb, pos, route, modseg, 5, norm_mix, i + 1, mod_all[i + 1], final=False)
        else:
            y = combine(x, yb, pos, route, modseg, 5, final_norm.reshape(1, d), 0, modseg, final=True)

    return (y[:T_PROMPT].reshape(BATCH, SEQ, d), y[T_PROMPT:].reshape(DEC_BATCH, DEC_SEQ, d),
            jnp.stack(new_hg, axis=1), jnp.stack(new_dk, axis=1), jnp.stack(new_dv, axis=1),
            jnp.stack(new_nk, axis=1), jnp.stack(new_nv, axis=1))
```

```python
import functools
import math

import jax
import jax.numpy as jnp
from jax import lax
from jax.experimental import pallas as pl
from jax.experimental.pallas import tpu as pltpu

D_MODEL = 2048
BATCH = 32
SEQ = 256
DEPTH = 4
DEC_BATCH = 4
DEC_SEQ = 4096
PAST_LEN = 512
GRID_W = 64
N_MIXERS = 3
HEADS = 16
HEAD_DIM = D_MODEL // HEADS
DA_QK_DIM = 64
NA_WIN_ROWS = 8
NA_WIN_COLS = 16
MOE_GROUPS = 4
MOE_EXPERTS_PER_GROUP = 8
MOE_EXPERTS = MOE_GROUPS * MOE_EXPERTS_PER_GROUP
MOE_D_EXPERT = 512
ROPE_BASE = 10000.0
NORM_EPS = 1e-6
N_MOD = 6

T_PROMPT = BATCH * SEQ
T_SAMPLE = DEC_BATCH * DEC_SEQ
T_ALL = T_PROMPT + T_SAMPLE
SEG = DEC_SEQ
N_SEG = T_ALL // SEG
MOD_ROWS = 8

LANES = 128
VMEM_LIMIT = 56 * 1024 * 1024

TM = 512
TN = 1024
MOE_ROWS = 512
HG_TILE = 128
HG_HEADS_LONG = 2
HG_HEADS_SHORT = 8
NA_TILE_ROWS = 8
NA_KEY_ROWS = 16

F32 = jnp.float32
BF16 = jnp.bfloat16


def _cparams(n_axes):
    return pltpu.CompilerParams(dimension_semantics=("arbitrary",) * n_axes, vmem_limit_bytes=VMEM_LIMIT)


def _split_bf16(x):
    hi = x.astype(BF16)
    lo = (x - hi.astype(F32)).astype(BF16)
    return hi, lo


def _pack_rows(x):
    n = x.shape[1] // 2
    hi = lax.bitcast_convert_type(x[:, 0:n].astype(BF16).astype(F32), jnp.uint32)
    lo = lax.bitcast_convert_type(x[:, n:2 * n].astype(BF16).astype(F32), jnp.uint32)
    return hi | (lo >> 16)


def _unpack_rows(u):
    hi = lax.bitcast_convert_type(u & jnp.uint32(0xFFFF0000), F32)
    lo = lax.bitcast_convert_type(u << 16, F32)
    return jnp.concatenate([hi, lo], axis=1)


def _ada_kernel(cond_ref, w_ref, b_ref, o_ref):
    a = cond_ref[...]
    a = a / (1.0 + jnp.exp(-a))
    a_hi, a_lo = _split_bf16(a)
    w_hi, w_lo = _split_bf16(w_ref[...])
    acc = jnp.dot(a_hi, w_hi, preferred_element_type=F32)
    acc += jnp.dot(a_lo, w_hi, preferred_element_type=F32)
    acc += jnp.dot(a_hi, w_lo, preferred_element_type=F32)
    o_ref[...] = acc + b_ref[...]


def ada_all(cond8, w_ada, b_ada):
    depth, d, n = w_ada.shape
    tn = 1024
    return pl.pallas_call(
        _ada_kernel,
        grid=(depth, n // tn),
        in_specs=[
            pl.BlockSpec((MOD_ROWS, d), lambda l, j: (0, 0)),
            pl.BlockSpec((None, d, tn), lambda l, j: (l, 0, j)),
            pl.BlockSpec((None, 1, tn), lambda l, j: (l, 0, j)),
        ],
        out_specs=pl.BlockSpec((None, MOD_ROWS, tn), lambda l, j: (l, 0, j)),
        out_shape=jax.ShapeDtypeStruct((depth, MOD_ROWS, n), F32),
        compiler_params=_cparams(2),
        name="ada",
    )(cond8, w_ada, b_ada.reshape(depth, 1, n))


def _norm_mod(x, gain, mod, sh_idx, sc_idx):
    ms = jnp.mean(x * x, axis=-1, keepdims=True)
    y = x * lax.rsqrt(ms + NORM_EPS) * gain
    return y * (1.0 + mod[sc_idx:sc_idx + 1, :]) + mod[sh_idx:sh_idx + 1, :]


def _norm_kernel(x_ref, gain_ref, mod_ref, h_ref, *, sh_idx, sc_idx):
    h_ref[...] = _norm_mod(x_ref[...], gain_ref[...], mod_ref[...], sh_idx, sc_idx).astype(BF16)


def _route(logits):
    lane = lax.broadcasted_iota(jnp.int32, logits.shape, 1)
    neg = jnp.float32(-jnp.inf)
    big = jnp.int32(LANES)
    lg = jnp.where(lane < MOE_GROUPS, logits, neg)
    mg = jnp.max(lg, axis=1, keepdims=True)
    grp = jnp.min(jnp.where(lg == mg, lane, big), axis=1, keepdims=True)
    p_grp = 1.0 / jnp.sum(jnp.exp(lg - mg), axis=1, keepdims=True)
    lo = MOE_GROUPS + grp * MOE_EXPERTS_PER_GROUP
    le = jnp.where((lane >= lo) & (lane < lo + MOE_EXPERTS_PER_GROUP), logits, neg)
    m1 = jnp.max(le, axis=1, keepdims=True)
    i1 = jnp.min(jnp.where(le == m1, lane, big), axis=1, keepdims=True)
    le2 = jnp.where(lane == i1, neg, le)
    m2 = jnp.max(le2, axis=1, keepdims=True)
    i2 = jnp.min(jnp.where(le2 == m2, lane, big), axis=1, keepdims=True)
    p2 = jnp.exp(m2 - m1)
    g1 = p_grp / (1.0 + p2)
    g2 = p_grp * p2 / (1.0 + p2)
    return g1, g2, i1 - MOE_GROUPS, i2 - MOE_GROUPS


def _norm_route_kernel(x_ref, gain_ref, mod_ref, wr_hi_ref, wr_lo_ref, br_ref, h_ref, r_ref, cnt_ref, run_ref,
                       *, sh_idx, sc_idx):
    @pl.when(pl.program_id(0) == 0)
    def _():
        run_ref[...] = jnp.zeros_like(run_ref)

    h = _norm_mod(x_ref[...], gain_ref[...], mod_ref[...], sh_idx, sc_idx)
    h_ref[...] = _pack_rows(h)
    h_hi, h_lo = _split_bf16(h)
    w_hi = wr_hi_ref[...]
    logits = jnp.dot(h_hi, w_hi, preferred_element_type=F32)
    logits += jnp.dot(h_lo, w_hi, preferred_element_type=F32)
    logits += jnp.dot(h_hi, wr_lo_ref[...], preferred_element_type=F32)
    g1, g2, e1, e2 = _route(logits + br_ref[...])

    tm = h.shape[0]
    lane = lax.broadcasted_iota(jnp.int32, (tm, LANES), 1)
    o1 = (lane == e1).astype(F32)
    o2 = (lane == e2).astype(F32)
    before = (lax.broadcasted_iota(jnp.int32, (tm, tm), 0) > lax.broadcasted_iota(jnp.int32, (tm, tm), 1)).astype(BF16)
    p1 = jnp.dot(before, o1.astype(BF16), preferred_element_type=F32)
    p2 = jnp.dot(before, o2.astype(BF16), preferred_element_type=F32)
    tot1 = jnp.sum(o1, axis=0, keepdims=True)
    tot2 = jnp.sum(o2, axis=0, keepdims=True)
    run = run_ref[0:1, :]
    rank1 = jnp.sum(o1 * (run + p1), axis=1, keepdims=True)
    rank2 = jnp.sum(o2 * (run + tot1 + p2), axis=1, keepdims=True)
    run = run + tot1 + tot2
    run_ref[0:1, :] = run
    cnt_ref[...] = jnp.broadcast_to(run, cnt_ref.shape)

    out = jnp.where(lane == 0, g1, 0.0)
    out = jnp.where(lane == 1, g2, out)
    out = jnp.where(lane == 2, e1.astype(F32), out)
    out = jnp.where(lane == 3, e2.astype(F32), out)
    out = jnp.where(lane == 4, rank1, out)
    out = jnp.where(lane == 5, rank2, out)
    r_ref[...] = out


def norm_mod(x, gains, layer, modseg, sh_idx, sc_idx, router=None):
    t, d = x.shape
    in_specs = [
        pl.BlockSpec((TM, d), lambda i: (i, 0)),
        pl.BlockSpec((None, 1, d), lambda i: (layer, 0, 0)),
        pl.BlockSpec((None, MOD_ROWS, d), lambda i: (i * TM // SEG, 0, 0)),
    ]
    h_spec = pl.BlockSpec((TM, d), lambda i: (i, 0))
    h_shape = jax.ShapeDtypeStruct((t, d), BF16)
    gains3 = gains.reshape(gains.shape[0], 1, d)
    if router is None:
        return pl.pallas_call(
            functools.partial(_norm_kernel, sh_idx=sh_idx, sc_idx=sc_idx),
            grid=(t // TM,), in_specs=in_specs, out_specs=h_spec, out_shape=h_shape,
            compiler_params=_cparams(1), name="norm_mod",
        )(x, gains3, modseg)
    wr_hi, wr_lo, br = router
    const = lambda i: (0, 0)
    return pl.pallas_call(
        functools.partial(_norm_route_kernel, sh_idx=sh_idx, sc_idx=sc_idx),
        grid=(t // TM,),
        in_specs=in_specs + [pl.BlockSpec((d, LANES), const), pl.BlockSpec((d, LANES), const),
                             pl.BlockSpec((1, LANES), const)],
        out_specs=[pl.BlockSpec((TM, d // 2), lambda i: (i, 0)), pl.BlockSpec((TM, LANES), lambda i: (i, 0)),
                   pl.BlockSpec((MOD_ROWS, LANES), const)],
        out_shape=[jax.ShapeDtypeStruct((t, d // 2), jnp.uint32), jax.ShapeDtypeStruct((t, LANES), F32),
                   jax.ShapeDtypeStruct((MOD_ROWS, LANES), F32)],
        scratch_shapes=[pltpu.VMEM((MOD_ROWS, LANES), F32)],
        compiler_params=_cparams(1), name="norm_route",
    )(x, gains3, modseg, wr_hi, wr_lo, br)


def _mm_kernel(h_ref, w_ref, o_ref, wbf_ref):
    @pl.when(pl.program_id(1) == 0)
    def _():
        wbf_ref[...] = w_ref[...].astype(BF16)

    o_ref[...] = jnp.dot(h_ref[...], wbf_ref[...], preferred_element_type=F32).astype(o_ref.dtype)


def _mm_resid_kernel(hp_ref, hs_ref, w_ref, x_ref, mod_ref, o_ref, wbf_ref, *, g_idx, n_prompt_tiles):
    i = pl.program_id(1)

    @pl.when(i == 0)
    def _():
        wbf_ref[...] = w_ref[...].astype(BF16)

    def finish(h):
        acc = jnp.dot(h, wbf_ref[...], preferred_element_type=F32)
        o_ref[...] = x_ref[...] + mod_ref[g_idx:g_idx + 1, :] * acc

    @pl.when(i < n_prompt_tiles)
    def _():
        finish(hp_ref[...])

    @pl.when(i >= n_prompt_tiles)
    def _():
        finish(hs_ref[...])


def matmul(h, w, layer, col0, n_cols, out_dtype):
    n_rows, k = h.shape
    tn = min(TN, n_cols)
    c0 = col0 // tn
    return pl.pallas_call(
        _mm_kernel,
        grid=(n_cols // tn, n_rows // TM),
        in_specs=[pl.BlockSpec((TM, k), lambda j, i: (i, 0)),
                  pl.BlockSpec((None, k, tn), lambda j, i: (layer, 0, c0 + j))],
        out_specs=pl.BlockSpec((TM, tn), lambda j, i: (i, j)),
        out_shape=jax.ShapeDtypeStruct((n_rows, n_cols), out_dtype),
        scratch_shapes=[pltpu.VMEM((k, tn), BF16)],
        compiler_params=_cparams(2), name="matmul",
    )(h, w)


def matmul_resid(hp, hs, w, layer, x, modseg, g_idx):
    k = hp.shape[1]
    t = hp.shape[0] + hs.shape[0]
    n = w.shape[-1]
    tn = min(TN, n)
    npt = hp.shape[0] // TM
    return pl.pallas_call(
        functools.partial(_mm_resid_kernel, g_idx=g_idx, n_prompt_tiles=npt),
        grid=(n // tn, t // TM),
        in_specs=[pl.BlockSpec((TM, k), lambda j, i: (jnp.minimum(i, npt - 1), 0)),
                  pl.BlockSpec((TM, k), lambda j, i: (jnp.maximum(i - npt, 0), 0)),
                  pl.BlockSpec((None, k, tn), lambda j, i: (layer, 0, j)),
                  pl.BlockSpec((TM, tn), lambda j, i: (i, j)),
                  pl.BlockSpec((None, MOD_ROWS, tn), lambda j, i: (i * TM // SEG, 0, j))],
        out_specs=pl.BlockSpec((TM, tn), lambda j, i: (i, j)),
        out_shape=jax.ShapeDtypeStruct((t, n), F32),
        scratch_shapes=[pltpu.VMEM((k, tn), BF16)],
        compiler_params=_cparams(2), name="matmul_resid",
    )(hp, hs, w, x, modseg)


def _rope(x, cos, sin):
    lane = lax.broadcasted_iota(jnp.int32, x.shape, 1)
    first_half = (lane % DA_QK_DIM) < (DA_QK_DIM // 2)
    partner = jnp.where(first_half, pltpu.roll(x, LANES - DA_QK_DIM // 2, 1), pltpu.roll(x, DA_QK_DIM // 2, 1))
    return x * cos + partner * sin


def _store_values(vb_ref, v):
    vb_ref[:, 0:LANES] = v.astype(BF16)
    vb_ref[:, LANES:2 * LANES] = jnp.ones((v.shape[0], LANES), BF16)


def _softmax_values(q_bf, kb, vb, kcb, vcb, bias=None):
    dn = (((1,), (1,)), ((), ()))
    s = lax.dot_general(q_bf, kb, dn, preferred_element_type=F32)
    if bias is not None:
        s = s + bias
    m = jnp.max(s, axis=1, keepdims=True)
    if kcb is not None:
        sc = lax.dot_general(q_bf, kcb, dn, preferred_element_type=F32)
        m = jnp.maximum(m, jnp.max(sc, axis=1, keepdims=True))
    acc = jnp.dot(jnp.exp((s - m).astype(BF16)), vb, preferred_element_type=F32)
    if kcb is not None:
        acc += jnp.dot(jnp.exp((sc - m).astype(BF16)), vcb, preferred_element_type=F32)
    return acc[:, 0:LANES], acc[:, LANES:2 * LANES]


def _softmax_parts(q_bf, kb):
    s = lax.dot_general(q_bf, kb, (((1,), (1,)), ((), ())), preferred_element_type=F32)
    p = jnp.exp(s - jnp.max(s, axis=1, keepdims=True))
    return p, jnp.sum(p, axis=1, keepdims=True)


PROMPT_HEADS_PER_STEP = 4


def _attn_kernel(*refs, diff, scale, out_scale):
    if diff:
        q_ref, k_ref, v_ref, lam_ref, g_ref, o_ref = refs
    else:
        q_ref, k_ref, v_ref, o_ref = refs
    for hh in range(PROMPT_HEADS_PER_STEP):
        lanes = slice(hh * HEAD_DIM, (hh + 1) * HEAD_DIM)
        q = q_ref[:, lanes].astype(F32) * scale
        kb = k_ref[:, lanes].astype(BF16)
        vb = v_ref[:, lanes].astype(BF16)
        if diff:
            lane = lax.broadcasted_iota(jnp.int32, q.shape, 1)
            p0, l0 = _softmax_parts(jnp.where(lane < DA_QK_DIM, q, 0.0).astype(BF16), kb)
            p1, l1 = _softmax_parts(jnp.where(lane >= DA_QK_DIM, q, 0.0).astype(BF16), kb)
            a = p0 * (1.0 / l0) - p1 * (lam_ref[0:1, 0:1] / l1)
            o = jnp.dot(a.astype(BF16), vb, preferred_element_type=F32)
            ms = jnp.mean(o * o, axis=-1, keepdims=True)
            o = (o * lax.rsqrt(ms + NORM_EPS) * g_ref[...]) * out_scale
        else:
            p, l = _softmax_parts(q.astype(BF16), kb)
            o = jnp.dot((p * (1.0 / l)).astype(BF16), vb, preferred_element_type=F32)
        o_ref[:, lanes] = o.astype(o_ref.dtype)


def attention(qkv, *, n_batch, n_seq, diff, lam=None, subln=None, out_scale=1.0, scale):
    width = PROMPT_HEADS_PER_STEP * HEAD_DIM
    groups = HEADS // PROMPT_HEADS_PER_STEP
    in_specs = [pl.BlockSpec((n_seq, width), lambda b, g: (b, g)),
                pl.BlockSpec((n_seq, width), lambda b, g: (b, groups + g)),
                pl.BlockSpec((n_seq, width), lambda b, g: (b, 2 * groups + g))]
    args = [qkv, qkv, qkv]
    if diff:
        in_specs += [pl.BlockSpec((1, LANES), lambda b, g: (0, 0))] * 2
        args += [lam, subln]
    return pl.pallas_call(
        functools.partial(_attn_kernel, diff=diff, scale=scale, out_scale=out_scale),
        grid=(n_batch, groups),
        in_specs=in_specs,
        out_specs=pl.BlockSpec((n_seq, width), lambda b, g: (b, g)),
        out_shape=jax.ShapeDtypeStruct((n_batch * n_seq, D_MODEL), BF16),
        compiler_params=_cparams(2), name="attention",
    )(*args)


LOG2_E = 1.4426950408889634


DA_KEY_CHUNK = 512
DA_SUM_ROWS = 16


def _diff_long_kernel(q_ref, k_ref, v_ref, ck_ref, cv_ref, cosq_ref, sinq_ref, cosk_ref, sink_ref, lam_ref, g_ref,
                      o_ref, kb_ref, vt_ref, s_ref, p_ref, *, n_lat, n_ctx, scale, out_scale):
    n_k = n_lat + n_ctx
    n_chunks = n_k // DA_KEY_CHUNK
    dv = v_ref.shape[1]

    @pl.when(pl.program_id(2) == 0)
    def _():
        kb_ref[0:n_lat, :] = _rope(k_ref[...].astype(F32), cosk_ref[...], sink_ref[...]).astype(BF16)
        kb_ref[n_lat:n_k, :] = ck_ref[...].astype(BF16)
        vt_ref[0:dv, 0:n_lat] = v_ref[...].astype(F32).T.astype(BF16)
        vt_ref[0:dv, n_lat:n_k] = cv_ref[...].astype(F32).T.astype(BF16)
        vt_ref[dv:dv + DA_SUM_ROWS, :] = jnp.ones((DA_SUM_ROWS, n_k), BF16)

    q = _rope(q_ref[...].astype(F32), cosq_ref[...], sinq_ref[...]) * (scale * LOG2_E)
    lane = lax.broadcasted_iota(jnp.int32, q.shape, 1)
    qs = (jnp.where(lane < DA_QK_DIM, q, 0.0).astype(BF16), jnp.where(lane >= DA_QK_DIM, q, 0.0).astype(BF16))
    dn = (((1,), (1,)), ((), ()))
    chunks = [pl.ds(j * DA_KEY_CHUNK, DA_KEY_CHUNK) for j in range(n_chunks)]

    def scores(c, rows, m):
        st = lax.dot_general(kb_ref[rows, :], qs[c], dn, preferred_element_type=F32)
        s_ref[c, rows, :] = st
        cm = jnp.max(st, axis=0, keepdims=True)
        return cm if m is None else jnp.maximum(m, cm)

    def exps(c, rows, m):
        p_ref[c, rows, :] = jnp.exp2((s_ref[c, rows, :] - m).astype(BF16))

    def values(c, rows, acc):
        pv = jnp.dot(vt_ref[:, rows], p_ref[c, rows, :], preferred_element_type=F32)
        return pv if acc is None else acc + pv

    m0 = m1 = acc0 = acc1 = None
    for rows in chunks:
        m0 = scores(0, rows, m0)
    for rows in chunks:
        m1 = scores(1, rows, m1)
        exps(0, rows, m0)
    for rows in chunks:
        exps(1, rows, m1)
        acc0 = values(0, rows, acc0)
    for rows in chunks:
        acc1 = values(1, rows, acc1)
    o_t = acc0[0:dv, :] / acc0[dv:dv + 1, :] - lam_ref[0:1, 0:1] * (acc1[0:dv, :] / acc1[dv:dv + 1, :])
    o = o_t.T
    ms = jnp.mean(o * o, axis=-1, keepdims=True)
    o_ref[...] = ((o * lax.rsqrt(ms + NORM_EPS) * g_ref[...]) * out_scale).astype(o_ref.dtype)


def diff_attention_long(qkv, *, row0, n_batch, n_seq, tq, ctx, rope_tabs, lam, subln, out_scale, scale):
    nq = n_seq // tq
    qb0, kb0 = row0 // tq, row0 // n_seq
    ck, cv, cl = ctx
    n_ctx = ck.shape[2]
    cos, sin = rope_tabs
    cspec = pl.BlockSpec((None, None, n_ctx, LANES), lambda b, h, i: (b, cl, 0, h))
    qtab = pl.BlockSpec((tq, LANES), lambda b, h, i: (i, 0))
    ktab = pl.BlockSpec((n_seq, LANES), lambda b, h, i: (0, 0))
    row = pl.BlockSpec((1, LANES), lambda b, h, i: (0, 0))
    return pl.pallas_call(
        functools.partial(_diff_long_kernel, n_lat=n_seq, n_ctx=n_ctx, scale=scale, out_scale=out_scale),
        grid=(n_batch, HEADS, nq),
        in_specs=[pl.BlockSpec((tq, LANES), lambda b, h, i: (qb0 + b * nq + i, h)),
                  pl.BlockSpec((n_seq, LANES), lambda b, h, i: (kb0 + b, HEADS + h)),
                  pl.BlockSpec((n_seq, LANES), lambda b, h, i: (kb0 + b, 2 * HEADS + h)),
                  cspec, cspec, qtab, qtab, ktab, ktab, row, row],
        out_specs=pl.BlockSpec((tq, LANES), lambda b, h, i: (b * nq + i, h)),
        out_shape=jax.ShapeDtypeStruct((n_batch * n_seq, D_MODEL), BF16),
        scratch_shapes=[pltpu.VMEM((n_seq + n_ctx, LANES), BF16),
                        pltpu.VMEM((LANES + DA_SUM_ROWS, n_seq + n_ctx), BF16),
                        pltpu.VMEM((2, n_seq + n_ctx, tq), F32), pltpu.VMEM((2, n_seq + n_ctx, tq), BF16)],
        compiler_params=_cparams(3), name="diff_attention_long",
    )(qkv, qkv, qkv, ck, cv, cos, sin, cos, sin, lam, subln)


def _na_key_row0(t):
    return jnp.clip(t * NA_TILE_ROWS - NA_WIN_ROWS // 2, 0, DEC_SEQ // GRID_W - NA_KEY_ROWS)


def _na_kernel(q_ref, k_ref, v_ref, ck_ref, cv_ref, bias_ref, o_ref, kb_ref, vb_ref, kcb_ref, vcb_ref, *, scale):
    t = pl.program_id(2)

    @pl.when(t == 0)
    def _():
        kb_ref[...] = k_ref[...].astype(BF16)
        _store_values(vb_ref, v_ref[...])
        kcb_ref[...] = ck_ref[...].astype(BF16)
        _store_values(vcb_ref, cv_ref[...])

    n_keys = NA_KEY_ROWS * GRID_W
    k0 = pl.multiple_of(_na_key_row0(t) * GRID_W, GRID_W)
    kw = kb_ref[pl.ds(k0, n_keys), :]
    vw = vb_ref[pl.ds(k0, n_keys), :]
    q = (q_ref[...].astype(F32) * scale).astype(BF16)
    n0, l0 = _softmax_values(q, kw, vw, kcb_ref[...], vcb_ref[...], bias=bias_ref[...])
    o_ref[...] = (n0 / l0).astype(o_ref.dtype)


def na_bias_table(rpb):
    rows = DEC_SEQ // GRID_W
    n_h = rpb.shape[0]
    col = jnp.arange(GRID_W)
    col0 = jnp.clip(col - NA_WIN_COLS // 2, 0, GRID_W - NA_WIN_COLS)
    col_ok = (col[None, :] >= col0[:, None]) & (col[None, :] < col0[:, None] + NA_WIN_COLS)
    dc_idx = jnp.clip(col[None, :] - col[:, None], -(NA_WIN_COLS - 1), NA_WIN_COLS - 1) + NA_WIN_COLS - 1
    pick = (dc_idx[None] == jnp.arange(2 * NA_WIN_COLS - 1)[:, None, None]).astype(F32)
    band = jnp.einsum('hrc,cqk->hrqk', rpb.astype(F32), pick, precision=lax.Precision.HIGHEST)
    band = jnp.where(col_ok, band, -jnp.inf)
    masked = jnp.full((n_h, GRID_W, GRID_W), -jnp.inf, F32)
    classes = []
    for tile in (0, 1, rows // NA_TILE_ROWS - 1):
        u0 = min(max(tile * NA_TILE_ROWS - NA_WIN_ROWS // 2, 0), rows - NA_KEY_ROWS)
        q_rows = []
        for jq in range(NA_TILE_ROWS):
            r = tile * NA_TILE_ROWS + jq
            r0 = min(max(r - NA_WIN_ROWS // 2, 0), rows - NA_WIN_ROWS)
            blocks = [band[:, u0 + u - r + NA_WIN_ROWS - 1] if r0 <= u0 + u < r0 + NA_WIN_ROWS else masked
                      for u in range(NA_KEY_ROWS)]
            q_rows.append(jnp.concatenate(blocks, axis=-1))
        classes.append(jnp.concatenate(q_rows, axis=1))
    return jnp.stack(classes, axis=1)


def na_attention(qkv, row0, cache_k, cache_v, cl, bias, scale):
    tq = NA_TILE_ROWS * GRID_W
    nt = DEC_SEQ // tq
    qb0, kb0 = row0 // tq, row0 // DEC_SEQ
    n_ctx = cache_k.shape[2]
    cspec = pl.BlockSpec((None, None, n_ctx, LANES), lambda b, h, t: (b, cl, 0, h))
    scratch = [pltpu.VMEM((DEC_SEQ, LANES), BF16), pltpu.VMEM((DEC_SEQ, 2 * LANES), BF16),
               pltpu.VMEM((n_ctx, LANES), BF16), pltpu.VMEM((n_ctx, 2 * LANES), BF16)]
    return pl.pallas_call(
        functools.partial(_na_kernel, scale=scale),
        grid=(DEC_BATCH, HEADS, nt),
        in_specs=[
            pl.BlockSpec((tq, LANES), lambda b, h, t: (qb0 + b * nt + t, h)),
            pl.BlockSpec((DEC_SEQ, LANES), lambda b, h, t: (kb0 + b, HEADS + h)),
            pl.BlockSpec((DEC_SEQ, LANES), lambda b, h, t: (kb0 + b, 2 * HEADS + h)),
            cspec, cspec,
            pl.BlockSpec((None, None, tq, NA_KEY_ROWS * GRID_W),
                         lambda b, h, t: (h, jnp.minimum(t, 1) + t // (nt - 1), 0, 0)),
        ],
        out_specs=pl.BlockSpec((tq, LANES), lambda b, h, t: (b * nt + t, h)),
        out_shape=jax.ShapeDtypeStruct((T_SAMPLE, D_MODEL), BF16),
        scratch_shapes=scratch,
        compiler_params=_cparams(3), name="na_attention",
    )(qkv, qkv, qkv, cache_k, cache_v, bias)


HG_LEVELS = HG_TILE.bit_length() - 1


def hgrn_masks():
    t = jnp.arange(HG_TILE)
    pair, qsel = [], []
    for d in range(2):
        for lvl in range(HG_LEVELS):
            upper = ((t >> lvl) & 1) == 1
            is_q = jnp.logical_not(upper) if d else upper
            same = (t[:, None] >> (lvl + 1)) == (t[None, :] >> (lvl + 1))
            pair.append(same & is_q[:, None] & jnp.logical_not(is_q)[None, :])
            qsel.append(jnp.broadcast_to(is_q[:, None], (HG_TILE, HG_TILE)))
    shape = (2, HG_LEVELS, HG_TILE, HG_TILE)
    return jnp.stack(pair).astype(F32).reshape(shape), jnp.stack(qsel).astype(F32).reshape(shape)


def _hgrn_tile(q, kk, f, v_f32, st_bf, d, pair_ref, qsel_ref):
    c = q.shape[0]
    rev = d == 1
    dn = (((1,), (1,)), ((), ()))
    px = f
    py = jnp.ones_like(f)
    pt = f
    operands = []
    for lvl in range(HG_LEVELS):
        m = 1 << lvl
        operands.append(((q * px).astype(BF16), (kk * py).astype(BF16)))
        if m % 8 == 0:
            halves = lambda a: (a.reshape(c // (2 * m), 2, m, a.shape[1])[:, 0],
                                a.reshape(c // (2 * m), 2, m, a.shape[1])[:, 1])
            join = lambda lo, hi: jnp.stack([lo, hi], axis=1).reshape(c, lo.shape[-1])
            (x_lo, x_hi), (y_lo, y_hi), (t_lo, t_hi) = halves(px), halves(py), halves(pt)
            if rev:
                px, py = join(x_lo * t_hi, x_hi), join(y_lo, y_hi * t_lo)
            else:
                px, py = join(x_lo, x_hi * t_lo), join(y_lo * t_hi, y_hi)
            both = t_lo * t_hi
            pt = join(both, both)
        else:
            below = pltpu.roll(pt, m, 0)
            above = pltpu.roll(pt, c - m, 0)
            is_q = qsel_ref[d, lvl] > 0.5
            sib = jnp.where(is_q, above, below) if rev else jnp.where(is_q, below, above)
            grown = jnp.where(is_q, px, py) * sib
            px = jnp.where(is_q, grown, px)
            py = jnp.where(is_q, py, grown)
            pt = pt * sib
    prods = [lax.dot_general(qp, kp, dn, preferred_element_type=F32) for qp, kp in operands]
    att = prods[0] * pair_ref[d, 0]
    for lvl in range(1, HG_LEVELS):
        att = att + prods[lvl] * pair_ref[d, lvl]
    tot = pt[0:1, :]
    v_bf = v_f32.astype(BF16)
    diag = jnp.sum(q * kk, axis=1, keepdims=True)
    o = jnp.dot(att.astype(BF16), v_bf, preferred_element_type=F32) + diag * v_f32
    o += lax.dot_general((q * px).astype(BF16), st_bf, dn, preferred_element_type=F32)
    inc = jnp.dot(v_f32.T.astype(BF16), (kk * py).astype(BF16), preferred_element_type=F32)
    return o, inc, tot


def _hgrn_kernel(*refs, n, has_s0, emit_state, hps):
    it = iter(refs)
    q_ref, i_ref, zf_ref, zb_ref, g_ref = next(it), next(it), next(it), next(it), next(it)
    s0_ref = next(it) if has_s0 else None
    lb_ref, gn_ref, pair_ref, qsel_ref = next(it), next(it), next(it), next(it)
    o_ref = next(it)
    sfin_ref = next(it) if emit_state else None
    of_ref, ob_ref, st_ref = next(it), next(it), next(it)
    nt = n // HG_TILE
    heads = [slice(hh * HEAD_DIM, (hh + 1) * HEAD_DIM) for hh in range(hps)]

    def gates(z, lanes):
        lb, one_m_lb = lb_ref[0:1, lanes], lb_ref[1:2, lanes]
        e = jnp.exp(-jnp.abs(z))
        r = 1.0 / (1.0 + e)
        pos = z > 0
        f = lb + one_m_lb * (jnp.where(pos, 1.0, e) * r)
        kk = one_m_lb * (jnp.where(pos, e, 1.0) * r)
        return f, kk

    for d in range(2):
        for hh in range(hps):
            if has_s0:
                st_ref[d, hh] = s0_ref[d, hh].T
            else:
                st_ref[d, hh] = jnp.zeros((HEAD_DIM, HEAD_DIM), F32)

    def scan_step(step, carry):
        for hh, lanes in enumerate(heads):
            for d, (z_ref, out_ref) in enumerate(((zf_ref, of_ref), (zb_ref, ob_ref))):
                ti = (nt - 1 - step) if d else step
                rows = pl.ds(pl.multiple_of(ti * HG_TILE, HG_TILE), HG_TILE)
                f, kk = gates(z_ref[rows, lanes], lanes)
                qr = q_ref[rows, lanes].astype(F32)
                q = (qr / (1.0 + jnp.exp(-qr))) * (HEAD_DIM ** -0.5)
                st = st_ref[d, hh]
                o, inc, tot = _hgrn_tile(q, kk, f, i_ref[rows, lanes].astype(F32), st.astype(BF16), d,
                                         pair_ref, qsel_ref)
                st_ref[d, hh] = st * tot + inc
                out_ref[rows, lanes] = o
        return carry

    lax.fori_loop(0, nt, scan_step, 0)

    def finish(ti, carry):
        rows = pl.ds(pl.multiple_of(ti * HG_TILE, HG_TILE), HG_TILE)
        for lanes in heads:
            o = of_ref[rows, lanes] + ob_ref[rows, lanes]
            ms = jnp.mean(o * o, axis=-1, keepdims=True)
            o = o * lax.rsqrt(ms + NORM_EPS) * gn_ref[...]
            g = g_ref[rows, lanes].astype(F32)
            o_ref[rows, lanes] = (o * (g / (1.0 + jnp.exp(-g)))).astype(o_ref.dtype)
        return carry

    lax.fori_loop(0, nt, finish, 0)
    if emit_state:
        for d in range(2):
            for hh in range(hps):
                sfin_ref[d, hh] = st_ref[d, hh].T


def hgrn_scan(qi, z, g, lbp, layer, gnorm, masks, *, row0, n_batch, n_seq, hps, s0=None, s0_layer=0,
              emit_state=False):
    rb0 = row0 // n_seq
    width = hps * HEAD_DIM
    groups = HEADS // hps
    blk = lambda c0: pl.BlockSpec((n_seq, width), lambda b, h: (rb0 + b, c0 + h))
    in_specs = [blk(0), blk(groups), blk(0), blk(groups), blk(0)]
    args = [qi, qi, z, z, g]
    if s0 is not None:
        in_specs.append(pl.BlockSpec((None, None, 2, hps, HEAD_DIM, HEAD_DIM),
                                     lambda b, h: (b, s0_layer, 0, h, 0, 0)))
        args.append(s0)
    mask_spec = pl.BlockSpec((2, HG_LEVELS, HG_TILE, HG_TILE), lambda b, h: (0, 0, 0, 0))
    in_specs += [pl.BlockSpec((None, MOD_ROWS, width), lambda b, h: (layer, 0, h)),
                 pl.BlockSpec((1, LANES), lambda b, h: (0, 0)), mask_spec, mask_spec]
    args += [lbp, gnorm, masks[0], masks[1]]
    out_specs = [pl.BlockSpec((n_seq, width), lambda b, h: (b, h))]
    out_shape = [jax.ShapeDtypeStruct((n_batch * n_seq, D_MODEL), BF16)]
    if emit_state:
        out_specs.append(pl.BlockSpec((None, 2, hps, HEAD_DIM, HEAD_DIM), lambda b, h: (b, 0, h, 0, 0)))
        out_shape.append(jax.ShapeDtypeStruct((n_batch, 2, HEADS, HEAD_DIM, HEAD_DIM), F32))
    res = pl.pallas_call(
        functools.partial(_hgrn_kernel, n=n_seq, has_s0=s0 is not None, emit_state=emit_state, hps=hps),
        grid=(n_batch, groups),
        in_specs=in_specs, out_specs=out_specs, out_shape=out_shape,
        scratch_shapes=[pltpu.VMEM((n_seq, width), F32), pltpu.VMEM((n_seq, width), F32),
                        pltpu.VMEM((2, hps, HEAD_DIM, HEAD_DIM), F32)],
        compiler_params=_cparams(2), name="hgrn_scan",
    )(*args)
    return res if emit_state else res[0]


def _expert_kernel(be_ref, first_ref, nused_ref, x_ref, wg_ref, wu_ref, wd_ref, o_ref, wg_bf, wu_bf, wd_bf):
    g = pl.program_id(0)

    @pl.when(g < nused_ref[0])
    def _():
        @pl.when(first_ref[g] == 1)
        def _():
            wg_bf[...] = wg_ref[...].astype(BF16)
            wu_bf[...] = wu_ref[...].astype(BF16)
            wd_bf[...] = wd_ref[...].astype(BF16)

        x = _unpack_rows(x_ref[...]).astype(BF16)
        a = jnp.dot(x, wg_bf[...], preferred_element_type=F32)
        u = jnp.dot(x, wu_bf[...], preferred_element_type=F32)
        hmid = (a / (1.0 + jnp.exp(-a))) * u
        o_ref[...] = _pack_rows(jnp.dot(hmid.astype(BF16), wd_bf[...], preferred_element_type=F32))

    @pl.when(g >= nused_ref[0])
    def _():
        o_ref[...] = jnp.zeros_like(o_ref)


def experts(xb, block_e, first, n_used, w_gate, w_up, w_down, layer):
    cap, dp = xb.shape
    d = 2 * dp
    nb = cap // MOE_ROWS
    de = w_gate.shape[-1]
    grid_spec = pltpu.PrefetchScalarGridSpec(
        num_scalar_prefetch=3,
        grid=(nb,),
        in_specs=[
            pl.BlockSpec((MOE_ROWS, dp), lambda g, be, fi, nu: (g, 0)),
            pl.BlockSpec((None, None, d, de), lambda g, be, fi, nu: (layer, be[g], 0, 0)),
            pl.BlockSpec((None, None, d, de), lambda g, be, fi, nu: (layer, be[g], 0, 0)),
            pl.BlockSpec((None, None, de, d), lambda g, be, fi, nu: (layer, be[g], 0, 0)),
        ],
        out_specs=pl.BlockSpec((MOE_ROWS, dp), lambda g, be, fi, nu: (g, 0)),
        scratch_shapes=[pltpu.VMEM((d, de), BF16), pltpu.VMEM((d, de), BF16), pltpu.VMEM((de, d), BF16)],
    )
    return pl.pallas_call(
        _expert_kernel, grid_spec=grid_spec,
        out_shape=jax.ShapeDtypeStruct((cap, dp), jnp.uint32),
        compiler_params=_cparams(1), name="experts",
    )(block_e, first, n_used, xb, w_gate, w_up, w_down)


def _dispatch_kernel(pos_ref, segend_ref, cnt_ref, nused_ref, h_ref, xb_hbm, zero_ref, sem, zsem):
    i = pl.program_id(0)
    rows = h_ref.shape[0]
    n_blocks = xb_hbm.shape[0] // MOE_ROWS

    def zero_block(start):
        return pltpu.make_async_copy(zero_ref, xb_hbm.at[pl.ds(pl.multiple_of(start, MOE_ROWS), MOE_ROWS), :], zsem)

    @pl.when(i == 0)
    def _():
        zero_ref[...] = jnp.zeros_like(zero_ref)

        def start(e, c):
            @pl.when(cnt_ref[e] > 0)
            def _():
                zero_block(segend_ref[e] - MOE_ROWS).start()
            return c

        def wait(e, c):
            @pl.when(cnt_ref[e] > 0)
            def _():
                zero_block(segend_ref[e] - MOE_ROWS).wait()
            return c

        def start_tail(g, c):
            zero_block(g * MOE_ROWS).start()
            return c

        def wait_tail(g, c):
            zero_block(g * MOE_ROWS).wait()
            return c

        lax.fori_loop(0, MOE_EXPERTS, start, 0)
        lax.fori_loop(nused_ref[0], n_blocks, start_tail, 0)
        lax.fori_loop(0, MOE_EXPERTS, wait, 0)
        lax.fori_loop(nused_ref[0], n_blocks, wait_tail, 0)

    def issue(r, c):
        src = h_ref.at[pl.ds(r, 1), :]
        pltpu.make_async_copy(src, xb_hbm.at[pl.ds(pos_ref[2 * r], 1), :], sem).start(priority=0)
        pltpu.make_async_copy(src, xb_hbm.at[pl.ds(pos_ref[2 * r + 1], 1), :], sem).start(priority=1)
        return c

    lax.fori_loop(0, rows, issue, 0, unroll=8)
    pltpu.make_async_copy(h_ref, xb_hbm.at[pl.ds(0, rows), :], sem).wait()
    pltpu.make_async_copy(h_ref, xb_hbm.at[pl.ds(0, rows), :], sem).wait()


def dispatch(h, pos_flat, seg_end, counts, n_used, cap):
    t, d = h.shape
    rows = 512
    grid_spec = pltpu.PrefetchScalarGridSpec(
        num_scalar_prefetch=0,
        grid=(t // rows,),
        in_specs=[pl.BlockSpec((2 * rows,), lambda i: (i,), memory_space=pltpu.SMEM),
                  pl.BlockSpec(memory_space=pltpu.SMEM),
                  pl.BlockSpec(memory_space=pltpu.SMEM),
                  pl.BlockSpec(memory_space=pltpu.SMEM),
                  pl.BlockSpec((rows, d), lambda i: (i, 0))],
        out_specs=pl.BlockSpec(memory_space=pl.ANY),
        scratch_shapes=[pltpu.VMEM((MOE_ROWS, d), h.dtype), pltpu.SemaphoreType.DMA(()),
                        pltpu.SemaphoreType.DMA(())],
    )
    return pl.pallas_call(
        _dispatch_kernel, grid_spec=grid_spec,
        out_shape=jax.ShapeDtypeStruct((cap, d), h.dtype),
        compiler_params=_cparams(1), name="dispatch",
    )(pos_flat, seg_end, counts, n_used, h)


def _combine_kernel(pos_ref, posn_ref, x_ref, r_ref, mod_ref, gain_ref, modn_ref, yb_hbm, *rest, g_idx, final):
    if final:
        y_ref, buf, sem = rest
    else:
        o_ref, h_ref, buf, sem = rest
    tm = x_ref.shape[0]
    i = pl.program_id(0)
    slot = i % 2

    def gather(p_ref, s):
        def issue(r, c):
            pltpu.make_async_copy(yb_hbm.at[pl.ds(p_ref[2 * r], 1), :], buf.at[s, 0, pl.ds(r, 1), :],
                                  sem.at[s]).start(priority=0)
            pltpu.make_async_copy(yb_hbm.at[pl.ds(p_ref[2 * r + 1], 1), :], buf.at[s, 1, pl.ds(r, 1), :],
                                  sem.at[s]).start(priority=1)
            return c

        lax.fori_loop(0, tm, issue, 0, unroll=8)

    @pl.when(i == 0)
    def _():
        gather(pos_ref, slot)

    @pl.when(i + 1 < pl.num_programs(0))
    def _():
        gather(posn_ref, 1 - slot)

    pltpu.make_async_copy(yb_hbm.at[pl.ds(0, tm), :], buf.at[slot, 0], sem.at[slot]).wait()
    pltpu.make_async_copy(yb_hbm.at[pl.ds(0, tm), :], buf.at[slot, 1], sem.at[slot]).wait()
    r = r_ref[...]
    y = r[:, 0:1] * _unpack_rows(buf[slot, 0]) + r[:, 1:2] * _unpack_rows(buf[slot, 1])
    x = x_ref[...] + mod_ref[g_idx:g_idx + 1, :] * y
    if final:
        ms = jnp.mean(x * x, axis=-1, keepdims=True)
        y_ref[...] = x * lax.rsqrt(ms + NORM_EPS) * gain_ref[...]
    else:
        o_ref[...] = x
        h_ref[...] = _norm_mod(x, gain_ref[...], modn_ref[...], 0, 1).astype(BF16)


def combine(x, yb, pos_flat, route, modseg, g_idx, gains, gain_idx, modseg_next, final):
    t, d = x.shape
    tm = 256
    row_spec = pl.BlockSpec((tm, d), lambda i: (i, 0))
    mod_spec = pl.BlockSpec((None, MOD_ROWS, d), lambda i: (i * tm // SEG, 0, 0))
    if final:
        out_specs, out_shape = row_spec, jax.ShapeDtypeStruct((t, d), F32)
    else:
        out_specs = [row_spec, row_spec]
        out_shape = [jax.ShapeDtypeStruct((t, d), F32), jax.ShapeDtypeStruct((t, d), BF16)]
    grid_spec = pltpu.PrefetchScalarGridSpec(
        num_scalar_prefetch=0,
        grid=(t // tm,),
        in_specs=[pl.BlockSpec((2 * tm,), lambda i: (i,), memory_space=pltpu.SMEM),
                  pl.BlockSpec((2 * tm,), lambda i: (jnp.minimum(i + 1, t // tm - 1),), memory_space=pltpu.SMEM),
                  row_spec,
                  pl.BlockSpec((tm, LANES), lambda i: (i, 0)),
                  mod_spec,
                  pl.BlockSpec((None, 1, d), lambda i: (gain_idx, 0, 0)),
                  mod_spec,
                  pl.BlockSpec(memory_space=pl.ANY)],
        out_specs=out_specs,
        scratch_shapes=[pltpu.VMEM((2, 2, tm, d // 2), jnp.uint32), pltpu.SemaphoreType.DMA((2,))],
    )
    return pl.pallas_call(
        functools.partial(_combine_kernel, g_idx=g_idx, final=final), grid_spec=grid_spec, out_shape=out_shape,
        compiler_params=_cparams(1), name="combine",
    )(pos_flat, pos_flat, x, route, modseg, gains.reshape(gains.shape[0], 1, d), modseg_next, yb)


def moe_plan(route, cnt):
    t = route.shape[0]
    counts = cnt[0, :MOE_EXPERTS].astype(jnp.int32)
    padded = (counts + MOE_ROWS - 1) // MOE_ROWS * MOE_ROWS
    seg_end = jnp.cumsum(padded).astype(jnp.int32)
    seg_start = seg_end - padded
    e = route[:, 2:4].astype(jnp.int32)
    rank = route[:, 4:6].astype(jnp.int32)
    onehot = e[:, :, None] == jnp.arange(MOE_EXPERTS, dtype=jnp.int32)
    pos = rank + jnp.sum(jnp.where(onehot, seg_start, 0), axis=-1)
    n_blocks = -(-(2 * t + MOE_EXPERTS * (MOE_ROWS - 1)) // MOE_ROWS)
    blk0 = jnp.arange(n_blocks, dtype=jnp.int32) * MOE_ROWS
    block_e = jnp.minimum(jnp.sum(seg_end[None, :] <= blk0[:, None], axis=1), MOE_EXPERTS - 1).astype(jnp.int32)
    first = jnp.concatenate([jnp.ones((1,), jnp.int32), (block_e[1:] != block_e[:-1]).astype(jnp.int32)])
    n_used = (seg_end[-1] // MOE_ROWS).astype(jnp.int32).reshape(1)
    return pos.reshape(-1).astype(jnp.int32), seg_end, counts, block_e, first, n_used, n_blocks * MOE_ROWS


def _rope_tables(n, dim):
    quarter = dim // 4
    inv = jnp.power(ROPE_BASE, -jnp.arange(quarter, dtype=F32) / quarter)
    t = jnp.arange(n)
    row = (t // GRID_W).astype(F32)
    col = (t % GRID_W).astype(F32)
    ang = jnp.concatenate([row[:, None] * inv, col[:, None] * inv], axis=-1)
    cos, sin = jnp.cos(ang), jnp.sin(ang)
    reps = LANES // dim
    cos_l = jnp.tile(jnp.concatenate([cos, cos], axis=-1), (1, reps))
    sin_l = jnp.tile(jnp.concatenate([-sin, sin], axis=-1), (1, reps))
    return cos_l, sin_l


def kernel(x_prompt, x_sample, state_hgrn, cache_diff_k, cache_diff_v, cache_na_k, cache_na_v, c, c_ctx, norm_mix, norm_ffn, w_ada, b_ada, router_group_w, router_group_b, router_expert_w, router_expert_b, expert_w_gate, expert_w_up, expert_w_down, hg_w_in, hg_lb_logits, hg_gnorm, hg_w_out, da_w_qkv, da_lambda, da_subln, da_w_out, na_w_qkv, na_rpb, na_w_out, final_norm):
    d = D_MODEL
    x = jnp.concatenate([x_prompt.reshape(T_PROMPT, d), x_sample.reshape(T_SAMPLE, d)], axis=0)

    cond8 = jnp.zeros((MOD_ROWS, d), F32).at[0].set(c_ctx).at[1:1 + DEC_BATCH].set(c)
    mod_all = ada_all(cond8, w_ada, b_ada)
    seg_cond = jnp.array([0] * (T_PROMPT // SEG) + list(range(1, 1 + DEC_BATCH)), jnp.int32)
    mod_all = mod_all.reshape(DEPTH, MOD_ROWS, N_MOD, d)[:, seg_cond]
    mod_all = jnp.pad(mod_all, ((0, 0), (0, 0), (0, MOD_ROWS - N_MOD), (0, 0)))

    p = jax.nn.softmax(hg_lb_logits.astype(F32), axis=0)
    cum = jnp.cumsum(p, axis=0)
    lb = cum - cum[0:1]
    lbp = jnp.stack([lb, 1.0 - lb], axis=1)
    lbp = jnp.pad(lbp, ((0, 0), (0, MOD_ROWS - 2), (0, 0)))

    wr = jnp.concatenate([router_group_w, router_expert_w], axis=-1)
    wr = jnp.pad(wr, ((0, 0), (0, 0), (0, LANES - wr.shape[-1])))
    wr_hi = wr.astype(BF16)
    wr_lo = (wr - wr_hi.astype(F32)).astype(BF16)
    br = jnp.concatenate([router_group_b, router_expert_b], axis=-1).astype(F32)
    br = jnp.pad(br, ((0, 0), (0, LANES - br.shape[-1])))[:, None, :]

    hg_masks = hgrn_masks()
    cos_l, sin_l = _rope_tables(DEC_SEQ, DA_QK_DIM)
    cdk = cache_diff_k.reshape(DEC_BATCH, -1, PAST_LEN, d)
    cdv = cache_diff_v.reshape(DEC_BATCH, -1, PAST_LEN, d)
    cnk = cache_na_k.reshape(DEC_BATCH, -1, PAST_LEN, d)
    cnv = cache_na_v.reshape(DEC_BATCH, -1, PAST_LEN, d)

    new_hg, new_dk, new_dv, new_nk, new_nv = [], [], [], [], []
    h = norm_mod(x, norm_mix, 0, mod_all[0], 0, 1)
    for i in range(DEPTH):
        kind, j = i % N_MIXERS, i // N_MIXERS
        modseg = mod_all[i]
        if kind == 0:
            qi = matmul(h, hg_w_in, j, 0, 2 * d, BF16)
            z = matmul(h, hg_w_in, j, 2 * d, 2 * d, F32)
            g = matmul(h, hg_w_in, j, 4 * d, d, BF16)
            gn = hg_gnorm[j].reshape(1, HEAD_DIM)
            op, s_fin = hgrn_scan(qi, z, g, lbp, i, gn, hg_masks, row0=0, n_batch=BATCH, n_seq=SEQ,
                                  hps=HG_HEADS_SHORT, emit_state=True)
            os_ = hgrn_scan(qi, z, g, lbp, i, gn, hg_masks, row0=T_PROMPT, n_batch=DEC_BATCH, n_seq=DEC_SEQ,
                            hps=HG_HEADS_LONG,
                            s0=state_hgrn, s0_layer=j)
            new_hg.append(s_fin)
            w_out = hg_w_out
        elif kind == 1:
            lam_init = 0.8 - 0.6 * math.exp(-0.3 * i)
            lp = da_lambda[j].astype(F32)
            lam = jnp.exp(jnp.sum(lp[0] * lp[1])) - jnp.exp(jnp.sum(lp[2] * lp[3])) + lam_init
            lam_row = jnp.full((1, LANES), lam, F32)
            subln = da_subln[j].reshape(1, HEAD_DIM).astype(F32)
            qkv = matmul(h, da_w_qkv, j, 0, 3 * d, F32)
            scale = DA_QK_DIM ** -0.5
            op = attention(qkv, n_batch=BATCH, n_seq=SEQ, diff=True, lam=lam_row, subln=subln,
                           out_scale=1.0 - lam_init, scale=scale)
            os_ = diff_attention_long(qkv, row0=T_PROMPT, n_batch=DEC_BATCH, n_seq=DEC_SEQ, tq=512,
                                      ctx=(cdk, cdv, j), rope_tabs=(cos_l, sin_l), lam=lam_row, subln=subln,
                                      out_scale=1.0 - lam_init, scale=scale)
            new_dk.append(qkv[:T_PROMPT, d:2 * d].reshape(BATCH, SEQ, HEADS, 2, DA_QK_DIM))
            new_dv.append(qkv[:T_PROMPT, 2 * d:].reshape(BATCH, SEQ, HEADS, HEAD_DIM))
            w_out = da_w_out
        else:
            qkv = matmul(h, na_w_qkv, j, 0, 3 * d, F32)
            scale = HEAD_DIM ** -0.5
            op = attention(qkv, n_batch=BATCH, n_seq=SEQ, diff=False, scale=scale)
            os_ = na_attention(qkv, T_PROMPT, cnk, cnv, j, na_bias_table(na_rpb[j]), scale)
            new_nk.append(qkv[:T_PROMPT, d:2 * d].reshape(BATCH, SEQ, HEADS, HEAD_DIM))
            new_nv.append(qkv[:T_PROMPT, 2 * d:].reshape(BATCH, SEQ, HEADS, HEAD_DIM))
            w_out = na_w_out
        x = matmul_resid(op, os_, w_out, j, x, modseg, 2)

        hf, route, cnt = norm_mod(x, norm_ffn, i, modseg, 3, 4, router=(wr_hi[i], wr_lo[i], br[i]))
        pos, seg_end, counts, block_e, first, n_used, cap = moe_plan(route, cnt)
        xb = dispatch(hf, pos, seg_end, counts, n_used, cap)
        yb = experts(xb, block_e, first, n_used, expert_w_gate, expert_w_up, expert_w_down, i)
        if i + 1 < DEPTH:
            x, h = combine(x, yb, pos, route, modseg, 5, norm_mix, i + 1, mod_all[i + 1], final=False)
        else:
            y = combine(x, yb, pos, route, modseg, 5, final_norm.reshape(1, d), 0, modseg, final=True)

    return (y[:T_PROMPT].reshape(BATCH, SEQ, d), y[T_PROMPT:].reshape(DEC_BATCH, DEC_SEQ, d),
            jnp.stack(new_hg, axis=1), jnp.stack(new_dk, axis=1), jnp.stack(new_dv, axis=1),
            jnp.stack(new_nk, axis=1), jnp.stack(new_nv, axis=1))
```
